```python
import math
import jax, jax.numpy as jnp
from jax import lax
import numpy as np

D_MODEL = 2048
BATCH = 4
SEQ = 2048
DEPTH = 4
DEC_BATCH = 8
DEC_SEQ = 4
PAST_LEN = 16384
PAGE_SIZE = 128

N_MIXERS = 2
N_SB_LAYERS = (DEPTH + N_MIXERS - 1) // N_MIXERS
N_HG_LAYERS = DEPTH // N_MIXERS
SB_HEAD_DIM = 128
SB_HEADS = D_MODEL // SB_HEAD_DIM
SB_BLOCK = 128
SB_BIAS_INIT = -6.0
HG_EXPAND = 128
HG_HEADS = D_MODEL // HG_EXPAND
HG_KEY_DIM = HG_EXPAND
HG_VAL_DIM = D_MODEL // HG_HEADS
HG_CHUNK = 64
D_FF = ((8 * D_MODEL // 3 + 255) // 256) * 256
FFN_CONV_W = 3
NORM_EPS = 1e-6

kernel_name = "hybrid_stickbreak_hgrn2_convffn_step"


def rms_norm(x, gain):
    xf = x.astype(jnp.float32)
    y = xf * lax.rsqrt(jnp.mean(xf * xf, axis=-1, keepdims=True) + NORM_EPS)
    return (y * gain.astype(jnp.float32)).astype(x.dtype)


def _stick_breaking_block(q, k, v, logit_bias, q_pos0):
    tq, tk = q.shape[1], k.shape[1]
    z = jnp.einsum("bqhd,bkhd->bhqk", q.astype(jnp.float32), k.astype(jnp.float32)) * (SB_HEAD_DIM ** -0.5)
    z = z + logit_bias.astype(jnp.float32)[None, :, None, None]
    t_pos = q_pos0 + jnp.arange(tq)[:, None]
    s_pos = jnp.arange(tk)[None, :]
    mask = s_pos < t_pos
    log_keep = jnp.where(mask, -jax.nn.softplus(z), 0.0)
    tail = lax.cumsum(log_keep, axis=3, reverse=True) - log_keep
    a = jnp.where(mask, jnp.exp(jax.nn.log_sigmoid(z) + tail), 0.0)
    return jnp.einsum("bhqk,bkhd->bqhd", a, v.astype(jnp.float32))


def stick_breaking_attention(q, k_all, v_all, logit_bias, pos0):
    tq = q.shape[1]
    outs = []
    for start in range(0, tq, SB_BLOCK):
        end = min(start + SB_BLOCK, tq)
        n_keys = pos0 + end
        outs.append(_stick_breaking_block(q[:, start:end], k_all[:, :n_keys], v_all[:, :n_keys],
                                          logit_bias, pos0 + start))
    return jnp.concatenate(outs, axis=1)


def sb_mixer(h, w_qkv, q_gain, k_gain, logit_bias, w_o, k_past, v_past, pos0):
    b, t, _ = h.shape
    q, k, v = jnp.split(h @ w_qkv, 3, axis=-1)
    q = rms_norm(q.reshape(b, t, SB_HEADS, SB_HEAD_DIM), q_gain)
    k = rms_norm(k.reshape(b, t, SB_HEADS, SB_HEAD_DIM), k_gain)
    v = v.reshape(b, t, SB_HEADS, SB_HEAD_DIM)
    if k_past is None:
        k_all, v_all = k, v
    else:
        k_all = jnp.concatenate([k_past.astype(k.dtype), k], axis=1)
        v_all = jnp.concatenate([v_past.astype(v.dtype), v], axis=1)
    o = stick_breaking_attention(q, k_all, v_all, logit_bias, pos0)
    y = o.reshape(b, t, D_MODEL).astype(h.dtype) @ w_o
    return y, k, v


def hgrn2_recurrence(q, k, v, log_f, s0):
    b, t, h, dk = q.shape
    c = math.gcd(t, HG_CHUNK)
    n = t // c

    def to_chunks(a):
        return a.reshape(b, n, c, h, a.shape[-1]).transpose(1, 0, 3, 2, 4)

    causal = jnp.tril(jnp.ones((c, c), dtype=bool))[:, :, None]

    def step(s, inp):
        qc, kc, vc, gc = inp
        g_cum = jnp.cumsum(gc, axis=2)
        inter = jnp.einsum("bhtd,bhde->bhte", qc * jnp.exp(g_cum), s)
        diff = g_cum[:, :, :, None, :] - g_cum[:, :, None, :, :]
        decay = jnp.exp(jnp.where(causal, diff, -jnp.inf))
        scores = jnp.einsum("bhtd,bhsd,bhtsd->bhts", qc, kc, decay)
        intra = jnp.einsum("bhts,bhse->bhte", scores, vc)
        g_last = g_cum[:, :, -1]
        k_dec = kc * jnp.exp(g_last[:, :, None, :] - g_cum)
        s_new = jnp.exp(g_last)[..., None] * s + jnp.einsum("bhsd,bhse->bhde", k_dec, vc)
        return s_new, inter + intra

    s_final, outs = lax.scan(step, s0, (to_chunks(q), to_chunks(k), to_chunks(v), to_chunks(log_f)))
    o = outs.transpose(1, 0, 3, 2, 4).reshape(b, t, h, v.shape[-1])
    return o, s_final


def hgrn2_mixer(h, w_in, lower_bound, out_gain, w_o, s0):
    b, t, _ = h.shape
    q, f, i, g = jnp.split(h @ w_in, 4, axis=-1)
    f = f.astype(jnp.float32)
    lb = lower_bound.astype(jnp.float32)
    log_forget = jnp.logaddexp(jnp.log(lb), jnp.log1p(-lb) + jax.nn.log_sigmoid(f))
    key = (1.0 - lb) * jax.nn.sigmoid(-f)
    o, s_new = hgrn2_recurrence(
        q.astype(jnp.float32).reshape(b, t, HG_HEADS, HG_KEY_DIM),
        key.reshape(b, t, HG_HEADS, HG_KEY_DIM),
        i.astype(jnp.float32).reshape(b, t, HG_HEADS, HG_VAL_DIM),
        log_forget.reshape(b, t, HG_HEADS, HG_KEY_DIM),
        s0.astype(jnp.float32))
    o = rms_norm(o, out_gain).reshape(b, t, D_MODEL)
    y = (o * jax.nn.silu(g.astype(jnp.float32))).astype(h.dtype) @ w_o
    return y, s_new


def conv_ffn(h, w_in, conv_w, conv_b, w_out, buf):
    t = h.shape[1]
    a, bval = jnp.split(h @ w_in, 2, axis=-1)
    ext = jnp.concatenate([buf.astype(a.dtype), a], axis=1)
    c = conv_b
    for j in range(FFN_CONV_W):
        c = c + conv_w[j] * ext[:, j:j + t]
    y = (jax.nn.silu(c) * bval) @ w_out
    return y, ext[:, -(FFN_CONV_W - 1):]


def setup_inputs(seed: int = 0) -> dict:
    key = jax.random.key(seed)
    ks = jax.random.split(key, 24)
    f32 = jnp.float32
    n_pages = PAST_LEN // PAGE_SIZE
    n_used = DEC_BATCH * n_pages
    n_pool = n_used + max(1, n_used // 4)

    def nrm(k, shape, scale):
        return jax.random.normal(k, shape, f32) * scale

    perm = jax.random.permutation(ks[0], n_pool)
    page_table = perm[:n_used].reshape(DEC_BATCH, n_pages).astype(jnp.int32)
    cache_shape = (N_SB_LAYERS, n_pool, PAGE_SIZE, SB_HEADS, SB_HEAD_DIM)
    return {
        "x_prompt": nrm(ks[1], (BATCH, SEQ, D_MODEL), 1.0),
        "x_sample": nrm(ks[2], (DEC_BATCH, DEC_SEQ, D_MODEL), 1.0),
        "cache_sb_k": nrm(ks[3], cache_shape, 1.0),
        "cache_sb_v": nrm(ks[4], cache_shape, 1.0),
        "page_table": page_table,
        "state_hgrn": nrm(ks[5], (N_HG_LAYERS, DEC_BATCH, HG_HEADS, HG_KEY_DIM, HG_VAL_DIM), 0.3),
        "state_ffn_conv": nrm(ks[6], (DEPTH, DEC_BATCH, FFN_CONV_W - 1, D_FF), 1.0),
        "norm_mixer": 1.0 + nrm(ks[7], (DEPTH, D_MODEL), 0.02),
        "norm_ffn": 1.0 + nrm(ks[8], (DEPTH, D_MODEL), 0.02),
        "w_sb_qkv": nrm(ks[9], (N_SB_LAYERS, D_MODEL, 3 * D_MODEL), D_MODEL ** -0.5),
        "sb_q_gain": 1.0 + nrm(ks[10], (N_SB_LAYERS, SB_HEAD_DIM), 0.02),
        "sb_k_gain": 1.0 + nrm(ks[11], (N_SB_LAYERS, SB_HEAD_DIM), 0.02),
        "sb_logit_bias": SB_BIAS_INIT + nrm(ks[21], (N_SB_LAYERS, SB_HEADS), 0.5),
        "w_sb_o": nrm(ks[12], (N_SB_LAYERS, D_MODEL, D_MODEL), D_MODEL ** -0.5),
        "w_hg_in": nrm(ks[13], (N_HG_LAYERS, D_MODEL, 4 * D_MODEL), D_MODEL ** -0.5),
        "hg_lower_bounds": nrm(ks[14], (DEPTH, D_MODEL), 0.5),
        "hg_out_gain": 1.0 + nrm(ks[15], (N_HG_LAYERS, HG_VAL_DIM), 0.02),
        "w_hg_o": nrm(ks[16], (N_HG_LAYERS, D_MODEL, D_MODEL), D_MODEL ** -0.5),
        "w_ffn_in": nrm(ks[17], (DEPTH, D_MODEL, 2 * D_FF), D_MODEL ** -0.5),
        "ffn_conv_w": nrm(ks[18], (DEPTH, FFN_CONV_W, D_FF), FFN_CONV_W ** -0.5),
        "ffn_conv_b": nrm(ks[19], (DEPTH, D_FF), 0.02),
        "w_ffn_out": nrm(ks[20], (DEPTH, D_FF, D_MODEL), D_FF ** -0.5),
    }


def reference(x_prompt, x_sample, cache_sb_k, cache_sb_v, page_table, state_hgrn, state_ffn_conv,
              norm_mixer, norm_ffn, w_sb_qkv, sb_q_gain, sb_k_gain, sb_logit_bias, w_sb_o,
              w_hg_in, hg_lower_bounds, hg_out_gain, w_hg_o,
              w_ffn_in, ffn_conv_w, ffn_conv_b, w_ffn_out):
    xp, xs = x_prompt, x_sample
    bp, bs = xp.shape[0], xs.shape[0]
    n_pages = page_table.shape[1]
    past_len = n_pages * PAGE_SIZE

    lb_soft = jax.nn.softmax(hg_lower_bounds.astype(jnp.float32), axis=0)
    lb_table = jnp.cumsum(lb_soft, axis=0) - lb_soft[0]

    kp_list, vp_list, ks_list, vs_list = [], [], [], []
    sp_list, ss_list, cp_list, cs_list = [], [], [], []
    for layer in range(DEPTH):
        j = layer // N_MIXERS
        hp = rms_norm(xp, norm_mixer[layer])
        hs = rms_norm(xs, norm_mixer[layer])
        if layer % N_MIXERS == 0:
            mp, kp, vp = sb_mixer(hp, w_sb_qkv[j], sb_q_gain[j], sb_k_gain[j], sb_logit_bias[j], w_sb_o[j],
                                  None, None, 0)
            k_past = cache_sb_k[j][page_table].reshape(bs, past_len, SB_HEADS, SB_HEAD_DIM)
            v_past = cache_sb_v[j][page_table].reshape(bs, past_len, SB_HEADS, SB_HEAD_DIM)
            ms, ks, vs = sb_mixer(hs, w_sb_qkv[j], sb_q_gain[j], sb_k_gain[j], sb_logit_bias[j], w_sb_o[j],
                                  k_past, v_past, past_len)
            kp_list.append(kp); vp_list.append(vp); ks_list.append(ks); vs_list.append(vs)
        else:
            s0p = jnp.zeros((bp, HG_HEADS, HG_KEY_DIM, HG_VAL_DIM), jnp.float32)
            mp, sp = hgrn2_mixer(hp, w_hg_in[j], lb_table[layer], hg_out_gain[j], w_hg_o[j], s0p)
            ms, ss = hgrn2_mixer(hs, w_hg_in[j], lb_table[layer], hg_out_gain[j], w_hg_o[j], state_hgrn[j])
            sp_list.append(sp.astype(state_hgrn.dtype)); ss_list.append(ss.astype(state_hgrn.dtype))
        xp = xp + mp
        xs = xs + ms
        hp = rms_norm(xp, norm_ffn[layer])
        hs = rms_norm(xs, norm_ffn[layer])
        bufp = jnp.zeros((bp, FFN_CONV_W - 1, D_FF), xp.dtype)
        fp, cp = conv_ffn(hp, w_ffn_in[layer], ffn_conv_w[layer], ffn_conv_b[layer], w_ffn_out[layer], bufp)
        fs, cs = conv_ffn(hs, w_ffn_in[layer], ffn_conv_w[layer], ffn_conv_b[layer], w_ffn_out[layer], state_ffn_conv[layer])
        cp_list.append(cp); cs_list.append(cs)
        xp = xp + fp
        xs = xs + fs

    new_k_prompt = jnp.stack(kp_list, axis=0)
    new_v_prompt = jnp.stack(vp_list, axis=0)
    new_k_sample = jnp.stack(ks_list, axis=0)
    new_v_sample = jnp.stack(vs_list, axis=0)
    hgrn_state_prompt = jnp.stack(sp_list, axis=0)
    hgrn_state_sample = jnp.stack(ss_list, axis=0)
    ffn_conv_prompt = jnp.stack(cp_list, axis=0)
    ffn_conv_sample = jnp.stack(cs_list, axis=0)
    return (xp, xs, new_k_prompt, new_v_prompt, new_k_sample, new_v_sample,
            hgrn_state_prompt, hgrn_state_sample, ffn_conv_prompt, ffn_conv_sample)
```

```python
import functools

import jax
import jax.numpy as jnp
from jax import lax
from jax.experimental import pallas as pl
from jax.experimental.pallas import tpu as pltpu

F32 = jnp.float32
BF16 = jnp.bfloat16

HEAD_DIM = 128
SUBLANES = 8
NORM_EPS = 1e-6
FFN_CONV_W = 3
SB_BLOCK = 128
HG_CHUNK = 64
HG_SUB = 16
VMEM_LIMIT_BYTES = 56 * 1024 * 1024


def _params(*semantics):
    return pltpu.CompilerParams(dimension_semantics=semantics, vmem_limit_bytes=VMEM_LIMIT_BYTES)


def _pick(n, target, quantum=HEAD_DIM):
    best = None
    for d in range(quantum, min(n, target) + 1, quantum):
        if n % d == 0:
            best = d
    return best if best is not None else n


def _softplus(z):
    return jnp.maximum(z, 0.0) + jnp.log1p(jnp.exp(-jnp.abs(z)))


def _split_bf16(x, parts):
    out = []
    r = x
    for _ in range(parts - 1):
        p = r.astype(BF16)
        out.append(p)
        r = r - p.astype(F32)
    out.append(r.astype(BF16))
    return out


def _rmsnorm_kernel(x_ref, g_ref, o_ref):
    x = x_ref[...]
    ms = jnp.mean(x * x, axis=-1, keepdims=True)
    o_ref[...] = (x * lax.rsqrt(ms + NORM_EPS) * g_ref[...]).astype(o_ref.dtype)


def _rmsnorm(x, gains, layer):
    m, d = x.shape
    tm = _pick(m, 512, SUBLANES)
    return pl.pallas_call(
        _rmsnorm_kernel,
        out_shape=jax.ShapeDtypeStruct((m, d), BF16),
        grid=(m // tm,),
        in_specs=[pl.BlockSpec((tm, d), lambda i: (i, 0)),
                  pl.BlockSpec((None, 1, d), lambda i: (layer, 0, 0))],
        out_specs=pl.BlockSpec((tm, d), lambda i: (i, 0)),
        compiler_params=_params("arbitrary"),
        name="rmsnorm",
    )(x, gains[:, None, :])


def _matmul_kernel(x_ref, w_ref, *rest, has_res):
    if has_res:
        r_ref, o_ref, wb_ref = rest
    else:
        o_ref, wb_ref = rest

    @pl.when(pl.program_id(1) == 0)
    def _():
        wb_ref[...] = w_ref[...].astype(BF16)

    acc = jnp.dot(x_ref[...].astype(BF16), wb_ref[...], preferred_element_type=F32)
    if has_res:
        acc = acc + r_ref[...]
    o_ref[...] = acc.astype(o_ref.dtype)


def _matmul(x, w, layer, *, res=None, sections=1, tm_target=1024, tn_target=1024):
    m, k = x.shape
    n = w.shape[-1]
    ns = n // sections
    tm = _pick(m, tm_target, SUBLANES)
    tn = _pick(ns, tn_target)
    per = ns // tn
    in_specs = [pl.BlockSpec((tm, k), lambda j, i: (i, 0)),
                pl.BlockSpec((None, k, tn), lambda j, i: (layer, 0, j))]
    args = [x, w]
    if res is not None:
        assert sections == 1
        in_specs.append(pl.BlockSpec((tm, tn), lambda j, i: (i, j)))
        args.append(res)
    out = pl.pallas_call(
        functools.partial(_matmul_kernel, has_res=res is not None),
        out_shape=jax.ShapeDtypeStruct((sections, m, ns), F32),
        grid=(n // tn, m // tm),
        in_specs=in_specs,
        out_specs=pl.BlockSpec((None, tm, tn), lambda j, i: (j // per, i, j % per)),
        scratch_shapes=[pltpu.VMEM((k, tn), BF16)],
        compiler_params=_params("arbitrary", "arbitrary"),
        name="matmul",
    )(*args)
    return out


def _kv_heads_kernel(k_ref, v_ref, g_ref, kn_ref, knh_ref, vh_ref, *, heads):
    g = g_ref[...]
    tm = k_ref.shape[0]
    for h in range(heads):
        cols = slice(h * HEAD_DIM, (h + 1) * HEAD_DIM)
        x = k_ref[:, cols]
        ms = jnp.mean(x * x, axis=-1, keepdims=True)
        kn = x * lax.rsqrt(ms + NORM_EPS) * g
        kn_ref[:, cols] = kn
        knh_ref[pl.ds(h, tm, stride=heads), :] = kn
        vh_ref[pl.ds(h, tm, stride=heads), :] = v_ref[:, cols]


def _kv_heads(qkv3, gains, layer):
    _, m, d = qkv3.shape
    heads = d // HEAD_DIM
    tm = _pick(m, 256, SUBLANES)
    out = jax.ShapeDtypeStruct((m * heads, HEAD_DIM), F32)
    head_spec = pl.BlockSpec((tm * heads, HEAD_DIM), lambda i: (i, 0))
    return pl.pallas_call(
        functools.partial(_kv_heads_kernel, heads=heads),
        out_shape=(jax.ShapeDtypeStruct((m, d), F32), out, out),
        grid=(m // tm,),
        in_specs=[pl.BlockSpec((None, tm, d), lambda i: (1, i, 0)),
                  pl.BlockSpec((None, tm, d), lambda i: (2, i, 0)),
                  pl.BlockSpec((None, 1, HEAD_DIM), lambda i: (layer, 0, 0))],
        out_specs=(pl.BlockSpec((tm, d), lambda i: (i, 0)), head_spec, head_spec),
        compiler_params=_params("arbitrary"),
        name="kv_heads",
    )(qkv3, qkv3, gains[:, None, :])


def _suffix_ones(n):
    r = lax.broadcasted_iota(jnp.int32, (n, n), 0)
    c = lax.broadcasted_iota(jnp.int32, (n, n), 1)
    return jnp.where(r > c, 1.0, 0.0).astype(BF16)


def _nt_dot(a, b):
    return lax.dot_general(a, b, (((1,), (1,)), ((), ())), preferred_element_type=F32)


def _sb_weights(z, carry, u, mask):
    sp = _softplus(z)
    lk = -sp if mask is None else jnp.where(mask, -sp, 0.0)
    tail = carry
    for part in _split_bf16(lk, 2):
        tail = tail + jnp.dot(part, u, preferred_element_type=F32)
    a = jnp.exp(z - sp + tail)
    if mask is not None:
        a = jnp.where(mask, a, 0.0)
    return a, carry + jnp.sum(lk, axis=-1, keepdims=True)


def _normed_query(q, gain):
    ms = jnp.mean(q * q, axis=-1, keepdims=True)
    return q * lax.rsqrt(ms + NORM_EPS) * gain * (HEAD_DIM ** -0.5)


def _sb_prompt_kernel(bias_ref, q_ref, k_ref, v_ref, qg_ref, o_ref):
    h = pl.program_id(1)
    i = pl.program_id(2)
    bias = bias_ref[h]
    qb = _normed_query(q_ref[...], qg_ref[...]).astype(BF16)
    u = _suffix_ones(SB_BLOCK)
    row = lax.broadcasted_iota(jnp.int32, (SB_BLOCK, SB_BLOCK), 0)
    col = lax.broadcasted_iota(jnp.int32, (SB_BLOCK, SB_BLOCK), 1)

    def tile(j, carry, mask):
        start = pl.multiple_of(j * SB_BLOCK, SB_BLOCK)
        kb = k_ref[pl.ds(start, SB_BLOCK), :].astype(BF16)
        vb = v_ref[pl.ds(start, SB_BLOCK), :].astype(BF16)
        a, carry = _sb_weights(_nt_dot(qb, kb) + bias, carry, u, mask)
        return jnp.dot(a.astype(BF16), vb, preferred_element_type=F32), carry

    acc, carry = tile(i, jnp.zeros((SB_BLOCK, 1), F32), col < row)

    def body(step, state):
        acc, carry = state
        out, carry = tile(i - 1 - step, carry, None)
        return acc + out, carry

    acc, _ = lax.fori_loop(0, i, body, (acc, carry))
    o_ref[...] = acc.astype(o_ref.dtype)


def _sb_prompt(qkv3, kn, q_gain, bias, layer, batch):
    _, m, d = qkv3.shape
    t = m // batch
    heads = d // HEAD_DIM
    nq = t // SB_BLOCK
    return pl.pallas_call(
        _sb_prompt_kernel,
        out_shape=jax.ShapeDtypeStruct((m, d), BF16),
        grid=(batch, heads, nq),
        in_specs=[pl.BlockSpec(memory_space=pltpu.SMEM),
                  pl.BlockSpec((None, SB_BLOCK, HEAD_DIM), lambda b, h, i: (0, b * nq + i, h)),
                  pl.BlockSpec((t, HEAD_DIM), lambda b, h, i: (b, h)),
                  pl.BlockSpec((None, t, HEAD_DIM), lambda b, h, i: (2, b, h)),
                  pl.BlockSpec((None, 1, HEAD_DIM), lambda b, h, i: (layer, 0, 0))],
        out_specs=pl.BlockSpec((SB_BLOCK, HEAD_DIM), lambda b, h, i: (b * nq + i, h)),
        compiler_params=_params("arbitrary", "arbitrary", "arbitrary"),
        name="sb_prompt",
    )(bias[layer], qkv3, kn, qkv3, q_gain[:, None, :])


def _sb_sample_kernel(pt_ref, q_ref, kn_ref, vn_ref, qg_ref, bias_ref, kc_ref, vc_ref, o_ref,
                      qh_ref, acc_ref, carry_ref, kpad_ref, vpad_ref, *, heads, tq, page):
    del pt_ref
    p = pl.program_id(1)
    rows = heads * SUBLANES
    u = _suffix_ones(page)

    def head_rows(h):
        return slice(h * SUBLANES, (h + 1) * SUBLANES)

    def tile(k_ref, v_ref, mask):
        z = jnp.concatenate(
            [_nt_dot(qh_ref[head_rows(h), :].astype(BF16),
                     k_ref[pl.ds(h, page, stride=heads), :].astype(BF16))
             for h in range(heads)], axis=0)
        a, carry = _sb_weights(z + bias_ref[...], carry_ref[...], u, mask)
        carry_ref[...] = carry
        for h in range(heads):
            acc_ref[head_rows(h), :] += jnp.dot(
                a[head_rows(h), :].astype(BF16),
                v_ref[pl.ds(h, page, stride=heads), :].astype(BF16), preferred_element_type=F32)

    @pl.when(p == 0)
    def _new_tokens():
        qh_ref[...] = jnp.zeros_like(qh_ref)
        for h in range(heads):
            cols = slice(h * HEAD_DIM, (h + 1) * HEAD_DIM)
            qh_ref[h * SUBLANES:h * SUBLANES + tq, :] = _normed_query(q_ref[:, cols], qg_ref[...])
        kpad_ref[...] = jnp.zeros_like(kpad_ref)
        vpad_ref[...] = jnp.zeros_like(vpad_ref)
        kpad_ref[0:tq * heads, :] = kn_ref[...]
        vpad_ref[0:tq * heads, :] = vn_ref[...]
        acc_ref[...] = jnp.zeros_like(acc_ref)
        carry_ref[...] = jnp.zeros_like(carry_ref)
        qi = lax.broadcasted_iota(jnp.int32, (rows, page), 0) % SUBLANES
        s = lax.broadcasted_iota(jnp.int32, (rows, page), 1)
        tile(kpad_ref, vpad_ref, s < qi)

    @pl.when(p > 0)
    def _past_page():
        tile(kc_ref, vc_ref, None)

    @pl.when(p == pl.num_programs(1) - 1)
    def _finish():
        for h in range(heads):
            o_ref[:, h * HEAD_DIM:(h + 1) * HEAD_DIM] = acc_ref[h * SUBLANES:h * SUBLANES + tq, :]


def _sb_sample(qkv3, kn, vh, q_gain, bias, cache_k, cache_v, page_table, layer, batch):
    _, m, d = qkv3.shape
    tq = m // batch
    heads = d // HEAD_DIM
    assert tq <= SUBLANES
    rows = heads * SUBLANES
    n_pages = page_table.shape[1]
    page = cache_k.shape[2] // heads
    qkv4 = qkv3.reshape(3, batch, tq, d)
    bias_rows = jnp.broadcast_to(jnp.repeat(bias[layer], SUBLANES)[:, None], (rows, page)).astype(F32)

    def page_index(b, p, pt):
        return (layer, pt[b, n_pages - jnp.maximum(p, 1)], 0, 0)

    new_spec = pl.BlockSpec((None, tq * heads, HEAD_DIM), lambda b, p, pt: (b, 0, 0))
    grid_spec = pltpu.PrefetchScalarGridSpec(
        num_scalar_prefetch=1,
        grid=(batch, n_pages + 1),
        in_specs=[pl.BlockSpec((None, None, tq, d), lambda b, p, pt: (0, b, 0, 0)),
                  new_spec,
                  new_spec,
                  pl.BlockSpec((None, 1, HEAD_DIM), lambda b, p, pt: (layer, 0, 0)),
                  pl.BlockSpec((rows, page), lambda b, p, pt: (0, 0)),
                  pl.BlockSpec((None, None, page * heads, HEAD_DIM), page_index),
                  pl.BlockSpec((None, None, page * heads, HEAD_DIM), page_index)],
        out_specs=pl.BlockSpec((None, tq, d), lambda b, p, pt: (b, 0, 0)),
        scratch_shapes=[pltpu.VMEM((rows, HEAD_DIM), F32),
                        pltpu.VMEM((rows, HEAD_DIM), F32),
                        pltpu.VMEM((rows, 1), F32),
                        pltpu.VMEM((page * heads, HEAD_DIM), F32),
                        pltpu.VMEM((page * heads, HEAD_DIM), F32)])
    out = pl.pallas_call(
        functools.partial(_sb_sample_kernel, heads=heads, tq=tq, page=page),
        out_shape=jax.ShapeDtypeStruct((batch, tq, d), F32),
        grid_spec=grid_spec,
        compiler_params=_params("arbitrary", "arbitrary"),
        name="sb_sample",
    )(page_table, qkv4, kn.reshape(batch, tq * heads, HEAD_DIM), vh.reshape(batch, tq * heads, HEAD_DIM),
      q_gain[:, None, :], bias_rows, cache_k, cache_v)
    return out.reshape(m, d)


def _lower_bounds_kernel(x_ref, lb_ref):
    x = x_ref[...]
    e = jnp.exp(x - jnp.max(x, axis=0, keepdims=True))
    soft = e / jnp.sum(e, axis=0, keepdims=True)
    depth = x.shape[0]
    run = jnp.zeros_like(soft[0:1])
    for layer in range(depth):
        run = run + soft[layer:layer + 1]
        lb = run - soft[0:1]
        lb_ref[layer, 0:1, :] = lb
        lb_ref[layer, 1:2, :] = jnp.log(lb)
        lb_ref[layer, 2:3, :] = jnp.log1p(-lb)
        lb_ref[layer, 3:4, :] = 1.0 - lb
        lb_ref[layer, 4:8, :] = jnp.zeros((4, x.shape[1]), F32)


def _lower_bounds(hg_lower_bounds):
    depth, d = hg_lower_bounds.shape
    return pl.pallas_call(
        _lower_bounds_kernel,
        out_shape=jax.ShapeDtypeStruct((depth, SUBLANES, d), F32),
        name="hg_lower_bounds",
    )(hg_lower_bounds.astype(F32))


def _hg_gates(f, lbp):
    log_lb, log_1m, one_m = lbp[1:2], lbp[2:3], lbp[3:4]
    log_sig = jnp.minimum(f, 0.0) - jnp.log1p(jnp.exp(-jnp.abs(f)))
    b = log_1m + log_sig
    g = jnp.maximum(log_lb, b) + jnp.log1p(jnp.exp(-jnp.abs(log_lb - b)))
    key = one_m * (1.0 / (1.0 + jnp.exp(f)))
    return g, key


def _hg_output(o, gate, gain):
    ms = jnp.mean(o * o, axis=-1, keepdims=True)
    on = o * lax.rsqrt(ms + NORM_EPS) * gain
    return on * (gate * (1.0 / (1.0 + jnp.exp(-gate))))


def _hg_prompt_kernel(q_ref, f_ref, i_ref, g_ref, lbp_ref, gain_ref, o_ref, s_ref,
                      st_ref, gc_ref, *, chunk, sub):
    t = q_ref.shape[0]
    n_sub = chunk // sub
    lbp = lbp_ref[...]
    gain = gain_ref[...]
    r = lax.broadcasted_iota(jnp.int32, (chunk, chunk), 0)
    c = lax.broadcasted_iota(jnp.int32, (chunk, chunk), 1)
    lower = jnp.where(c <= r, 1.0, 0.0).astype(BF16)
    crow = lax.broadcasted_iota(jnp.int32, (chunk, HEAD_DIM), 0)
    srow = lax.broadcasted_iota(jnp.int32, (sub, HEAD_DIM), 0)
    trow = lax.broadcasted_iota(jnp.int32, (sub, sub), 0)
    st_ref[...] = jnp.zeros_like(st_ref)

    def body(ci, _):
        r0 = pl.multiple_of(ci * chunk, chunk)
        rows = pl.ds(r0, chunk)
        glog, kk = _hg_gates(f_ref[rows, :], lbp)
        gc = jnp.zeros((chunk, HEAD_DIM), F32)
        for part in _split_bf16(glog, 3):
            gc = gc + jnp.dot(lower, part, preferred_element_type=F32)
        gc_ref[...] = gc
        q = q_ref[rows, :]
        v = i_ref[rows, :]
        vb = v.astype(BF16)
        st = st_ref[...]
        inter = lax.dot_general((q * jnp.exp(gc)).astype(BF16), st.astype(BF16),
                                (((1,), (1,)), ((), ())), preferred_element_type=F32)
        g_last = gc[chunk - 1:chunk, :]
        for si in range(n_sub):
            base = si * sub
            blk = slice(base, base + sub)
            q_s, g_s, k_s = q[blk], gc[blk], kk[blk]
            o_s = inter[blk]
            if si > 0:
                g_start = gc[base - 1:base, :]
                k_prev = jnp.where(crow < base, kk * jnp.exp(jnp.minimum(g_start - gc, 0.0)), 0.0)
                q_in = q_s * jnp.exp(g_s - g_start)
                off = lax.dot_general(q_in.astype(BF16), k_prev.astype(BF16),
                                      (((1,), (1,)), ((), ())), preferred_element_type=F32)
                o_s = o_s + jnp.dot(off.astype(BF16), vb, preferred_element_type=F32)
            q_sb = q_s.astype(BF16)
            diag = jnp.zeros((sub, sub), F32)
            for ti in range(sub):
                g_t = gc_ref[base + ti:base + ti + 1, :]
                k_t = jnp.where(srow <= ti, k_s * jnp.exp(jnp.minimum(g_t - g_s, 0.0)), 0.0)
                sc = lax.dot_general(q_sb, k_t.astype(BF16), (((1,), (1,)), ((), ())),
                                     preferred_element_type=F32)
                diag = jnp.where(trow == ti, sc, diag)
            o_s = o_s + jnp.dot(diag.astype(BF16), vb[blk], preferred_element_type=F32)
            out_rows = pl.ds(pl.multiple_of(r0 + base, sub), sub)
            o_ref[out_rows, :] = _hg_output(o_s, g_ref[out_rows, :], gain).astype(o_ref.dtype)
        k_dec = kk * jnp.exp(g_last - gc)
        st_ref[...] = st * jnp.exp(g_last) + lax.dot_general(
            vb, k_dec.astype(BF16), (((0,), (0,)), ((), ())), preferred_element_type=F32)
        return 0

    lax.fori_loop(0, t // chunk, body, 0)
    s_ref[...] = st_ref[...].T


def _hg_prompt(proj, lbp, out_gain, layer, hg_index, batch):
    m, d4 = proj.shape
    d = d4 // 4
    heads = d // HEAD_DIM
    t = m // batch
    chunk = _pick(t, HG_CHUNK, HG_SUB)
    sub = HG_SUB
    col = lambda sec: (lambda b, h: (b, sec * heads + h))
    return pl.pallas_call(
        functools.partial(_hg_prompt_kernel, chunk=chunk, sub=sub),
        out_shape=(jax.ShapeDtypeStruct((m, d), BF16),
                   jax.ShapeDtypeStruct((batch, heads, HEAD_DIM, HEAD_DIM), F32)),
        grid=(batch, heads),
        in_specs=[pl.BlockSpec((t, HEAD_DIM), col(0)),
                  pl.BlockSpec((t, HEAD_DIM), col(1)),
                  pl.BlockSpec((t, HEAD_DIM), col(2)),
                  pl.BlockSpec((t, HEAD_DIM), col(3)),
                  pl.BlockSpec((None, SUBLANES, HEAD_DIM), lambda b, h: (layer, 0, h)),
                  pl.BlockSpec((None, 1, HEAD_DIM), lambda b, h: (hg_index, 0, 0))],
        out_specs=(pl.BlockSpec((t, HEAD_DIM), lambda b, h: (b, h)),
                   pl.BlockSpec((None, None, HEAD_DIM, HEAD_DIM), lambda b, h: (b, h, 0, 0))),
        scratch_shapes=[pltpu.VMEM((HEAD_DIM, HEAD_DIM), F32),
                        pltpu.VMEM((chunk, HEAD_DIM), F32)],
        compiler_params=_params("arbitrary", "arbitrary"),
        name="hg_prompt",
    )(proj, proj, proj, proj, lbp, out_gain[:, None, :])


def _hg_sample_kernel(q_ref, f_ref, i_ref, g_ref, lbp_ref, gain_ref, s0_ref, o_ref, s_ref, pad_ref):
    tq = q_ref.shape[0]
    glog, kk = _hg_gates(f_ref[...], lbp_ref[...])

    def columns(x):
        pad_ref[...] = jnp.zeros_like(pad_ref)
        pad_ref[0:tq, :] = x
        return pad_ref[...].T

    q_c = columns(q_ref[...])
    f_c = columns(jnp.exp(glog))
    k_c = columns(kk)
    v = i_ref[...]
    s = s0_ref[...]
    for ti in range(tq):
        s = f_c[:, ti:ti + 1] * s + k_c[:, ti:ti + 1] * v[ti:ti + 1, :]
        o = jnp.sum(q_c[:, ti:ti + 1] * s, axis=0, keepdims=True)
        o_ref[ti:ti + 1, :] = _hg_output(o, g_ref[ti:ti + 1, :], gain_ref[...])
    s_ref[...] = s


def _hg_sample(proj, lbp, out_gain, state, layer, hg_index, batch):
    m, d4 = proj.shape
    d = d4 // 4
    heads = d // HEAD_DIM
    tq = m // batch
    proj3 = proj.reshape(batch, tq, d4)
    col = lambda sec: (lambda b, h: (b, 0, sec * heads + h))
    out, s_new = pl.pallas_call(
        _hg_sample_kernel,
        out_shape=(jax.ShapeDtypeStruct((batch, tq, d), F32),
                   jax.ShapeDtypeStruct((batch, heads, HEAD_DIM, HEAD_DIM), F32)),
        grid=(batch, heads),
        in_specs=[pl.BlockSpec((None, tq, HEAD_DIM), col(0)),
                  pl.BlockSpec((None, tq, HEAD_DIM), col(1)),
                  pl.BlockSpec((None, tq, HEAD_DIM), col(2)),
                  pl.BlockSpec((None, tq, HEAD_DIM), col(3)),
                  pl.BlockSpec((None, SUBLANES, HEAD_DIM), lambda b, h: (layer, 0, h)),
                  pl.BlockSpec((None, 1, HEAD_DIM), lambda b, h: (hg_index, 0, 0)),
                  pl.BlockSpec((None, None, None, HEAD_DIM, HEAD_DIM), lambda b, h: (hg_index, b, h, 0, 0))],
        out_specs=(pl.BlockSpec((None, tq, HEAD_DIM), lambda b, h: (b, 0, h)),
                   pl.BlockSpec((None, None, HEAD_DIM, HEAD_DIM), lambda b, h: (b, h, 0, 0))),
        scratch_shapes=[pltpu.VMEM((HEAD_DIM, HEAD_DIM), F32)],
        compiler_params=_params("arbitrary", "arbitrary"),
        name="hg_sample",
    )(proj3, proj3, proj3, proj3, lbp, out_gain[:, None, :], state)
    return out.reshape(m, d), s_new


def _ffn_gate_kernel(a_ref, b_ref, buf_ref, w_ref, cb_ref, o_ref, st_ref, ext_ref):
    t = a_ref.shape[0]
    lead = SUBLANES - (FFN_CONV_W - 1)
    ext_ref[lead:SUBLANES, :] = buf_ref[...]
    ext_ref[SUBLANES:SUBLANES + t, :] = a_ref[...]
    c = cb_ref[...]
    for j in range(FFN_CONV_W):
        c = c + w_ref[j:j + 1, :] * ext_ref[lead + j:lead + j + t, :]
    o_ref[...] = (c * (1.0 / (1.0 + jnp.exp(-c))) * b_ref[...]).astype(o_ref.dtype)
    st_ref[...] = ext_ref[SUBLANES + t - (FFN_CONV_W - 1):SUBLANES + t, :]


def _ffn_gate(ab, buf, conv_w, conv_b, layer, buf_layer, batch, out_dtype):
    m, f2 = ab.shape
    f = f2 // 2
    t = m // batch
    tf = _pick(f, 512)
    nf = f // tf
    ab3 = ab.reshape(batch, t, f2)
    out, st = pl.pallas_call(
        _ffn_gate_kernel,
        out_shape=(jax.ShapeDtypeStruct((batch, t, f), out_dtype),
                   jax.ShapeDtypeStruct((batch, FFN_CONV_W - 1, f), F32)),
        grid=(batch, nf),
        in_specs=[pl.BlockSpec((None, t, tf), lambda b, j: (b, 0, j)),
                  pl.BlockSpec((None, t, tf), lambda b, j: (b, 0, nf + j)),
                  pl.BlockSpec((None, None, FFN_CONV_W - 1, tf), lambda b, j: (buf_layer, b, 0, j)),
                  pl.BlockSpec((None, FFN_CONV_W, tf), lambda b, j: (layer, 0, j)),
                  pl.BlockSpec((None, 1, tf), lambda b, j: (layer, 0, j))],
        out_specs=(pl.BlockSpec((None, t, tf), lambda b, j: (b, 0, j)),
                   pl.BlockSpec((None, FFN_CONV_W - 1, tf), lambda b, j: (b, 0, j))),
        scratch_shapes=[pltpu.VMEM((t + SUBLANES, tf), F32)],
        compiler_params=_params("arbitrary", "arbitrary"),
        name="ffn_gate",
    )(ab3, ab3, buf, conv_w, conv_b[:, None, :])
    return out.reshape(m, f), st


def kernel(x_prompt, x_sample, cache_sb_k, cache_sb_v, page_table, state_hgrn, state_ffn_conv,
           norm_mixer, norm_ffn, w_sb_qkv, sb_q_gain, sb_k_gain, sb_logit_bias, w_sb_o,
           w_hg_in, hg_lower_bounds, hg_out_gain, w_hg_o,
           w_ffn_in, ffn_conv_w, ffn_conv_b, w_ffn_out):
    bp, tp, d = x_prompt.shape
    bs, ts, _ = x_sample.shape
    depth = norm_mixer.shape[0]
    heads = d // HEAD_DIM
    n_mixers = 2
    f = w_ffn_in.shape[-1] // 2
    n_sb, pool, page = cache_sb_k.shape[:3]
    cache_k = cache_sb_k.reshape(n_sb, pool, page * heads, HEAD_DIM)
    cache_v = cache_sb_v.reshape(n_sb, pool, page * heads, HEAD_DIM)
    zero_buf = jnp.zeros((1, bp, FFN_CONV_W - 1, f), F32)
    lbp = _lower_bounds(hg_lower_bounds)

    xp = x_prompt.reshape(bp * tp, d)
    xs = x_sample.reshape(bs * ts, d)
    kp_l, vp_l, ks_l, vs_l, sp_l, ss_l, cp_l, cs_l = [], [], [], [], [], [], [], []
    for layer in range(depth):
        j = layer // n_mixers
        hp = _rmsnorm(xp, norm_mixer, layer)
        hs = _rmsnorm(xs, norm_mixer, layer)
        if layer % n_mixers == 0:
            qkv_p = _matmul(hp, w_sb_qkv, j, sections=3)
            qkv_s = _matmul(hs, w_sb_qkv, j, sections=3)
            kn_p, knh_p, vh_p = _kv_heads(qkv_p, sb_k_gain, j)
            _, knh_s, vh_s = _kv_heads(qkv_s, sb_k_gain, j)
            op = _sb_prompt(qkv_p, kn_p, sb_q_gain, sb_logit_bias, j, bp)
            os_ = _sb_sample(qkv_s, knh_s, vh_s, sb_q_gain, sb_logit_bias, cache_k, cache_v,
                             page_table, j, bs)
            xp = _matmul(op, w_sb_o, j, res=xp)[0]
            xs = _matmul(os_, w_sb_o, j, res=xs)[0]
            kp_l.append(knh_p.reshape(bp, tp, heads, HEAD_DIM))
            vp_l.append(vh_p.reshape(bp, tp, heads, HEAD_DIM))
            ks_l.append(knh_s.reshape(bs, ts, heads, HEAD_DIM))
            vs_l.append(vh_s.reshape(bs, ts, heads, HEAD_DIM))
        else:
            proj_p = _matmul(hp, w_hg_in, j)[0]
            proj_s = _matmul(hs, w_hg_in, j)[0]
            op, sp = _hg_prompt(proj_p, lbp, hg_out_gain, layer, j, bp)
            os_, ss = _hg_sample(proj_s, lbp, hg_out_gain, state_hgrn, layer, j, bs)
            xp = _matmul(op, w_hg_o, j, res=xp)[0]
            xs = _matmul(os_, w_hg_o, j, res=xs)[0]
            sp_l.append(sp)
            ss_l.append(ss)
        hp = _rmsnorm(xp, norm_ffn, layer)
        hs = _rmsnorm(xs, norm_ffn, layer)
        ab_p = _matmul(hp, w_ffn_in, layer)[0]
        ab_s = _matmul(hs, w_ffn_in, layer)[0]
        gp, cp = _ffn_gate(ab_p, zero_buf, ffn_conv_w, ffn_conv_b, layer, 0, bp, BF16)
        gs, cs = _ffn_gate(ab_s, state_ffn_conv, ffn_conv_w, ffn_conv_b, layer, layer, bs, F32)
        xp = _matmul(gp, w_ffn_out, layer, res=xp, tm_target=512, tn_target=512)[0]
        xs = _matmul(gs, w_ffn_out, layer, res=xs, tn_target=512)[0]
        cp_l.append(cp)
        cs_l.append(cs)

    return (xp.reshape(bp, tp, d), xs.reshape(bs, ts, d),
            jnp.stack(kp_l), jnp.stack(vp_l), jnp.stack(ks_l), jnp.stack(vs_l),
            jnp.stack(sp_l), jnp.stack(ss_l), jnp.stack(cp_l), jnp.stack(cs_l))
```

```python
import functools

import jax
import jax.numpy as jnp
from jax import lax
from jax.experimental import pallas as pl
from jax.experimental.pallas import tpu as pltpu

F32 = jnp.float32
BF16 = jnp.bfloat16

HEAD_DIM = 128
SUBLANES = 8
NORM_EPS = 1e-6
FFN_CONV_W = 3
SB_TILE = 256
SB_HEADS_PER_STEP = 4
HG_HEADS_PER_STEP = 4
HG_CHUNK = 64
HG_SUB = 16
VMEM_LIMIT_BYTES = 56 * 1024 * 1024


def _params(*semantics):
    return pltpu.CompilerParams(dimension_semantics=semantics, vmem_limit_bytes=VMEM_LIMIT_BYTES)


def _pick(n, target, quantum=HEAD_DIM):
    best = None
    for d in range(quantum, min(n, target) + 1, quantum):
        if n % d == 0:
            best = d
    return best if best is not None else n


def _softplus(z):
    return jnp.maximum(z, 0.0) + jnp.log(1.0 + jnp.exp(-jnp.abs(z)))


def _split_bf16(x, parts):
    out = []
    r = x
    for _ in range(parts - 1):
        p = r.astype(BF16)
        out.append(p)
        r = r - p.astype(F32)
    out.append(r.astype(BF16))
    return out


def _rmsnorm_kernel(x_ref, g_ref, o_ref):
    x = x_ref[...]
    ms = jnp.mean(x * x, axis=-1, keepdims=True)
    o_ref[...] = (x * lax.rsqrt(ms + NORM_EPS) * g_ref[...]).astype(o_ref.dtype)


def _rmsnorm(x, gains, layer):
    m, d = x.shape
    tm = _pick(m, 512, SUBLANES)
    return pl.pallas_call(
        _rmsnorm_kernel,
        out_shape=jax.ShapeDtypeStruct((m, d), BF16),
        grid=(m // tm,),
        in_specs=[pl.BlockSpec((tm, d), lambda i: (i, 0)),
                  pl.BlockSpec((None, 1, d), lambda i: (layer, 0, 0))],
        out_specs=pl.BlockSpec((tm, d), lambda i: (i, 0)),
        compiler_params=_params("arbitrary"),
        name="rmsnorm",
    )(x, gains[:, None, :])


def _matmul_kernel(x_ref, w_ref, *rest, has_res):
    if has_res:
        r_ref, o_ref, wb_ref = rest
    else:
        o_ref, wb_ref = rest

    @pl.when(pl.program_id(1) == 0)
    def _():
        wb_ref[...] = w_ref[...].astype(BF16)

    acc = jnp.dot(x_ref[...].astype(BF16), wb_ref[...], preferred_element_type=F32)
    if has_res:
        acc = acc + r_ref[...]
    o_ref[...] = acc.astype(o_ref.dtype)


def _matmul(x, w, layer, *, res=None, sections=1, tm_target=1024, tn_target=1024):
    m, k = x.shape
    n = w.shape[-1]
    ns = n // sections
    tm = _pick(m, tm_target, SUBLANES)
    tn = _pick(ns, tn_target)
    per = ns // tn
    in_specs = [pl.BlockSpec((tm, k), lambda j, i: (i, 0)),
                pl.BlockSpec((None, k, tn), lambda j, i: (layer, 0, j))]
    args = [x, w]
    if res is not None:
        assert sections == 1
        in_specs.append(pl.BlockSpec((tm, tn), lambda j, i: (i, j)))
        args.append(res)
    out = pl.pallas_call(
        functools.partial(_matmul_kernel, has_res=res is not None),
        out_shape=jax.ShapeDtypeStruct((sections, m, ns), F32),
        grid=(n // tn, m // tm),
        in_specs=in_specs,
        out_specs=pl.BlockSpec((None, tm, tn), lambda j, i: (j // per, i, j % per)),
        scratch_shapes=[pltpu.VMEM((k, tn), BF16)],
        compiler_params=_params("arbitrary", "arbitrary"),
        name="matmul",
    )(*args)
    return out


def _kv_heads_kernel(k_ref, v_ref, g_ref, kb_ref, vb_ref, knh_ref, vh_ref, *, heads):
    g = g_ref[...]
    tm = k_ref.shape[0]
    for h in range(heads):
        cols = slice(h * HEAD_DIM, (h + 1) * HEAD_DIM)
        x = k_ref[:, cols]
        ms = jnp.mean(x * x, axis=-1, keepdims=True)
        kn = x * lax.rsqrt(ms + NORM_EPS) * g
        v = v_ref[:, cols]
        kb_ref[:, cols] = kn.astype(BF16)
        vb_ref[:, cols] = v.astype(BF16)
        knh_ref[pl.ds(h, tm, stride=heads), :] = kn
        vh_ref[pl.ds(h, tm, stride=heads), :] = v


def _kv_heads(qkv3, gains, layer):
    _, m, d = qkv3.shape
    heads = d // HEAD_DIM
    tm = _pick(m, 256, SUBLANES)
    flat = jax.ShapeDtypeStruct((m, d), BF16)
    by_head = jax.ShapeDtypeStruct((m * heads, HEAD_DIM), F32)
    flat_spec = pl.BlockSpec((tm, d), lambda i: (i, 0))
    head_spec = pl.BlockSpec((tm * heads, HEAD_DIM), lambda i: (i, 0))
    return pl.pallas_call(
        functools.partial(_kv_heads_kernel, heads=heads),
        out_shape=(flat, flat, by_head, by_head),
        grid=(m // tm,),
        in_specs=[pl.BlockSpec((None, tm, d), lambda i: (1, i, 0)),
                  pl.BlockSpec((None, tm, d), lambda i: (2, i, 0)),
                  pl.BlockSpec((None, 1, HEAD_DIM), lambda i: (layer, 0, 0))],
        out_specs=(flat_spec, flat_spec, head_spec, head_spec),
        compiler_params=_params("arbitrary"),
        name="kv_heads",
    )(qkv3, qkv3, gains[:, None, :])


def _suffix_ones(n):
    r = lax.broadcasted_iota(jnp.int32, (n, n), 0)
    c = lax.broadcasted_iota(jnp.int32, (n, n), 1)
    return jnp.where(r > c, 1.0, 0.0).astype(BF16)


def _nt_dot(a, b):
    return lax.dot_general(a, b, (((1,), (1,)), ((), ())), preferred_element_type=F32)


def _sb_log_keep(z, mask):
    sp = _softplus(z)
    lk = -sp if mask is None else jnp.where(mask, -sp, 0.0)
    return lk, z - sp


def _sb_tail(lk, carry, u):
    tail = carry
    for part in _split_bf16(lk, 2):
        tail = tail + jnp.dot(part, u, preferred_element_type=F32)
    return tail


def _sb_weights(z, carry, u, mask):
    lk, log_beta = _sb_log_keep(z, mask)
    a = jnp.exp(log_beta + _sb_tail(lk, carry, u))
    if mask is not None:
        a = jnp.where(mask, a, 0.0)
    return a, carry + jnp.sum(lk, axis=-1, keepdims=True)


def _normed_query(q, gain):
    ms = jnp.mean(q * q, axis=-1, keepdims=True)
    return q * lax.rsqrt(ms + NORM_EPS) * gain * (HEAD_DIM ** -0.5)


def _sb_prompt_kernel(bias_ref, q_ref, k_ref, v_ref, qg_ref, o_ref, qb_ref, acc_ref, carry_ref,
                      *, group):
    hg = pl.program_id(1)
    i = pl.program_id(2)
    tile = q_ref.shape[0]
    u = _suffix_ones(tile)
    row = lax.broadcasted_iota(jnp.int32, (tile, tile), 0)
    col = lax.broadcasted_iota(jnp.int32, (tile, tile), 1)
    for g in range(group):
        cols = slice(g * HEAD_DIM, (g + 1) * HEAD_DIM)
        qb_ref[:, cols] = _normed_query(q_ref[:, cols], qg_ref[...]).astype(BF16)
    acc_ref[...] = jnp.zeros_like(acc_ref)
    carry_ref[...] = jnp.zeros_like(carry_ref)

    def key_tile(j, mask):
        keys = pl.ds(pl.multiple_of(j * tile, tile), tile)
        heads = [slice(g * HEAD_DIM, (g + 1) * HEAD_DIM) for g in range(group)]
        zs = [_nt_dot(qb_ref[:, cols], k_ref[keys, cols]) + bias_ref[hg * group + g]
              for g, cols in enumerate(heads)]
        keeps = [_sb_log_keep(z, mask) for z in zs]
        tails = [_sb_tail(lk, carry_ref[g], u) for g, (lk, _) in enumerate(keeps)]
        for g, cols in enumerate(heads):
            lk, log_beta = keeps[g]
            a = jnp.exp(log_beta + tails[g])
            if mask is not None:
                a = jnp.where(mask, a, 0.0)
            acc_ref[:, cols] += jnp.dot(a.astype(BF16), v_ref[keys, cols],
                                        preferred_element_type=F32)
            carry_ref[g] += jnp.sum(lk, axis=-1, keepdims=True)

    key_tile(i, col < row)

    def body(step, _):
        key_tile(i - 1 - step, None)
        return 0

    lax.fori_loop(0, i, body, 0)
    o_ref[...] = acc_ref[...].astype(o_ref.dtype)


def _sb_prompt(qkv3, kb, vb, q_gain, bias, layer, batch):
    _, m, d = qkv3.shape
    t = m // batch
    heads = d // HEAD_DIM
    tile = _pick(t, SB_TILE)
    nq = t // tile
    group = SB_HEADS_PER_STEP if heads % SB_HEADS_PER_STEP == 0 else 1
    width = group * HEAD_DIM
    kv_spec = pl.BlockSpec((t, width), lambda b, h, i: (b, h))
    return pl.pallas_call(
        functools.partial(_sb_prompt_kernel, group=group),
        out_shape=jax.ShapeDtypeStruct((m, d), BF16),
        grid=(batch, heads // group, nq),
        in_specs=[pl.BlockSpec(memory_space=pltpu.SMEM),
                  pl.BlockSpec((None, tile, width), lambda b, h, i: (0, b * nq + i, h)),
                  kv_spec,
                  kv_spec,
                  pl.BlockSpec((None, 1, HEAD_DIM), lambda b, h, i: (layer, 0, 0))],
        out_specs=pl.BlockSpec((tile, width), lambda b, h, i: (b * nq + i, h)),
        scratch_shapes=[pltpu.VMEM((tile, width), BF16),
                        pltpu.VMEM((tile, width), F32),
                        pltpu.VMEM((group, tile, 1), F32)],
        compiler_params=_params("arbitrary", "arbitrary", "arbitrary"),
        name="sb_prompt",
    )(bias[layer], qkv3, kb, vb, q_gain[:, None, :])


def _sb_sample_kernel(pt_ref, q_ref, kn_ref, vn_ref, qg_ref, bias_ref, kc_ref, vc_ref, o_ref,
                      qh_ref, acc_ref, carry_ref, kpad_ref, vpad_ref, *, heads, tq, page):
    del pt_ref
    p = pl.program_id(1)
    rows = heads * SUBLANES
    u = _suffix_ones(page)

    def head_rows(h):
        return slice(h * SUBLANES, (h + 1) * SUBLANES)

    def tile(k_ref, v_ref, mask):
        z = jnp.concatenate(
            [_nt_dot(qh_ref[head_rows(h), :].astype(BF16),
                     k_ref[pl.ds(h, page, stride=heads), :].astype(BF16))
             for h in range(heads)], axis=0)
        a, carry = _sb_weights(z + bias_ref[...], carry_ref[...], u, mask)
        carry_ref[...] = carry
        for h in range(heads):
            acc_ref[head_rows(h), :] += jnp.dot(
                a[head_rows(h), :].astype(BF16),
                v_ref[pl.ds(h, page, stride=heads), :].astype(BF16), preferred_element_type=F32)

    @pl.when(p == 0)
    def _new_tokens():
        qh_ref[...] = jnp.zeros_like(qh_ref)
        for h in range(heads):
            cols = slice(h * HEAD_DIM, (h + 1) * HEAD_DIM)
            qh_ref[h * SUBLANES:h * SUBLANES + tq, :] = _normed_query(q_ref[:, cols], qg_ref[...])
        kpad_ref[...] = jnp.zeros_like(kpad_ref)
        vpad_ref[...] = jnp.zeros_like(vpad_ref)
        kpad_ref[0:tq * heads, :] = kn_ref[...]
        vpad_ref[0:tq * heads, :] = vn_ref[...]
        acc_ref[...] = jnp.zeros_like(acc_ref)
        carry_ref[...] = jnp.zeros_like(carry_ref)
        qi = lax.broadcasted_iota(jnp.int32, (rows, page), 0) % SUBLANES
        s = lax.broadcasted_iota(jnp.int32, (rows, page), 1)
        tile(kpad_ref, vpad_ref, s < qi)

    @pl.when(p > 0)
    def _past_page():
        tile(kc_ref, vc_ref, None)

    @pl.when(p == pl.num_programs(1) - 1)
    def _finish():
        for h in range(heads):
            o_ref[:, h * HEAD_DIM:(h + 1) * HEAD_DIM] = acc_ref[h * SUBLANES:h * SUBLANES + tq, :]


def _sb_sample(qkv3, kn, vh, q_gain, bias, cache_k, cache_v, page_table, layer, batch):
    _, m, d = qkv3.shape
    tq = m // batch
    heads = d // HEAD_DIM
    assert tq <= SUBLANES
    rows = heads * SUBLANES
    n_pages = page_table.shape[1]
    page = cache_k.shape[2] // heads
    qkv4 = qkv3.reshape(3, batch, tq, d)
    bias_rows = jnp.broadcast_to(jnp.repeat(bias[layer], SUBLANES)[:, None], (rows, page)).astype(F32)

    def page_index(b, p, pt):
        return (layer, pt[b, n_pages - jnp.maximum(p, 1)], 0, 0)

    new_spec = pl.BlockSpec((None, tq * heads, HEAD_DIM), lambda b, p, pt: (b, 0, 0))
    grid_spec = pltpu.PrefetchScalarGridSpec(
        num_scalar_prefetch=1,
        grid=(batch, n_pages + 1),
        in_specs=[pl.BlockSpec((None, None, tq, d), lambda b, p, pt: (0, b, 0, 0)),
                  new_spec,
                  new_spec,
                  pl.BlockSpec((None, 1, HEAD_DIM), lambda b, p, pt: (layer, 0, 0)),
                  pl.BlockSpec((rows, page), lambda b, p, pt: (0, 0)),
                  pl.BlockSpec((None, None, page * heads, HEAD_DIM), page_index),
                  pl.BlockSpec((None, None, page * heads, HEAD_DIM), page_index)],
        out_specs=pl.BlockSpec((None, tq, d), lambda b, p, pt: (b, 0, 0)),
        scratch_shapes=[pltpu.VMEM((rows, HEAD_DIM), F32),
                        pltpu.VMEM((rows, HEAD_DIM), F32),
                        pltpu.VMEM((rows, 1), F32),
                        pltpu.VMEM((page * heads, HEAD_DIM), F32),
                        pltpu.VMEM((page * heads, HEAD_DIM), F32)])
    out = pl.pallas_call(
        functools.partial(_sb_sample_kernel, heads=heads, tq=tq, page=page),
        out_shape=jax.ShapeDtypeStruct((batch, tq, d), F32),
        grid_spec=grid_spec,
        compiler_params=_params("arbitrary", "arbitrary"),
        name="sb_sample",
    )(page_table, qkv4, kn.reshape(batch, tq * heads, HEAD_DIM), vh.reshape(batch, tq * heads, HEAD_DIM),
      q_gain[:, None, :], bias_rows, cache_k, cache_v)
    return out.reshape(m, d)


def _lower_bounds_kernel(x_ref, lb_ref):
    x = x_ref[...]
    e = jnp.exp(x - jnp.max(x, axis=0, keepdims=True))
    soft = e / jnp.sum(e, axis=0, keepdims=True)
    depth = x.shape[0]
    run = jnp.zeros_like(soft[0:1])
    for layer in range(depth):
        run = run + soft[layer:layer + 1]
        lb = run - soft[0:1]
        lb_ref[layer, 0:1, :] = lb
        lb_ref[layer, 1:2, :] = jnp.log(lb)
        lb_ref[layer, 2:3, :] = jnp.log1p(-lb)
        lb_ref[layer, 3:4, :] = 1.0 - lb
        lb_ref[layer, 4:8, :] = jnp.zeros((4, x.shape[1]), F32)


def _lower_bounds(hg_lower_bounds):
    depth, d = hg_lower_bounds.shape
    return pl.pallas_call(
        _lower_bounds_kernel,
        out_shape=jax.ShapeDtypeStruct((depth, SUBLANES, d), F32),
        name="hg_lower_bounds",
    )(hg_lower_bounds.astype(F32))


def _hg_gates(f, lbp):
    log_lb, log_1m, one_m = lbp[1:2], lbp[2:3], lbp[3:4]
    log_sig = jnp.minimum(f, 0.0) - jnp.log1p(jnp.exp(-jnp.abs(f)))
    b = log_1m + log_sig
    g = jnp.maximum(log_lb, b) + jnp.log1p(jnp.exp(-jnp.abs(log_lb - b)))
    key = one_m * (1.0 / (1.0 + jnp.exp(f)))
    return g, key


def _hg_output(o, gate, gain):
    ms = jnp.mean(o * o, axis=-1, keepdims=True)
    on = o * lax.rsqrt(ms + NORM_EPS) * gain
    return on * (gate * (1.0 / (1.0 + jnp.exp(-gate))))


def _hg_prompt_kernel(q_ref, f_ref, i_ref, g_ref, lbp_ref, gain_ref, o_ref, s_ref,
                      st_ref, gc_ref, *, chunk, sub, group):
    t = q_ref.shape[0]
    n_sub = chunk // sub
    gain = gain_ref[...]
    r = lax.broadcasted_iota(jnp.int32, (chunk, chunk), 0)
    c = lax.broadcasted_iota(jnp.int32, (chunk, chunk), 1)
    lower = jnp.where(c <= r, 1.0, 0.0).astype(BF16)
    crow = lax.broadcasted_iota(jnp.int32, (chunk, HEAD_DIM), 0)
    srow = lax.broadcasted_iota(jnp.int32, (sub, HEAD_DIM), 0)
    own = (lax.broadcasted_iota(jnp.int32, (sub, sub * sub), 1) // sub
           == lax.broadcasted_iota(jnp.int32, (sub, sub * sub), 0))
    st_ref[...] = jnp.zeros_like(st_ref)

    def head_cols(hi):
        return slice(hi * HEAD_DIM, (hi + 1) * HEAD_DIM)

    def decay(r0, hi):
        glog, kk = _hg_gates(f_ref[pl.ds(r0, chunk), head_cols(hi)], lbp_ref[:, head_cols(hi)])
        gc = jnp.zeros((chunk, HEAD_DIM), F32)
        for part in _split_bf16(glog, 3):
            gc = gc + jnp.dot(lower, part, preferred_element_type=F32)
        return gc, kk

    def scores(r0, hi, gc, kk):
        rows = pl.ds(r0, chunk)
        gc_ref[hi] = gc
        q = q_ref[rows, head_cols(hi)]
        vb = i_ref[rows, head_cols(hi)].astype(BF16)
        st = st_ref[hi]
        inter = _nt_dot((q * jnp.exp(gc)).astype(BF16), st.astype(BF16))
        g_last = gc[chunk - 1:chunk, :]
        offs, diags = [], []
        for si in range(n_sub):
            base = si * sub
            blk = slice(base, base + sub)
            q_s, g_s, k_s = q[blk], gc[blk], kk[blk]
            if si > 0:
                g_start = gc[base - 1:base, :]
                k_prev = jnp.where(crow < base, kk * jnp.exp(jnp.minimum(g_start - gc, 0.0)), 0.0)
                q_in = q_s * jnp.exp(g_s - g_start)
                offs.append(_nt_dot(q_in.astype(BF16), k_prev.astype(BF16)))
            else:
                offs.append(None)
            stacked = []
            for ti in range(sub):
                g_t = gc_ref[hi, base + ti:base + ti + 1, :]
                k_t = jnp.where(srow <= ti, k_s * jnp.exp(jnp.minimum(g_t - g_s, 0.0)), 0.0)
                stacked.append(k_t.astype(BF16))
            diags.append(_nt_dot(q_s.astype(BF16), jnp.concatenate(stacked, axis=0)))
        k_dec = kk * jnp.exp(g_last - gc)
        st_ref[hi] = st * jnp.exp(g_last) + lax.dot_general(
            vb, k_dec.astype(BF16), (((0,), (0,)), ((), ())), preferred_element_type=F32)
        return vb, inter, offs, diags

    def outputs(r0, hi, vb, inter, offs, diags):
        for si in range(n_sub):
            base = si * sub
            blk = slice(base, base + sub)
            o_s = inter[blk]
            if offs[si] is not None:
                o_s = o_s + jnp.dot(offs[si].astype(BF16), vb, preferred_element_type=F32)
            sc = jnp.where(own, diags[si], 0.0).astype(BF16)
            o_s = o_s + jnp.dot(sc, jnp.concatenate([vb[blk]] * sub, axis=0),
                                preferred_element_type=F32)
            out_rows = pl.ds(pl.multiple_of(r0 + base, sub), sub)
            o_ref[out_rows, head_cols(hi)] = _hg_output(
                o_s, g_ref[out_rows, head_cols(hi)], gain).astype(o_ref.dtype)

    def body(ci, _):
        r0 = pl.multiple_of(ci * chunk, chunk)
        decays = [decay(r0, hi) for hi in range(group)]
        staged = [scores(r0, hi, *decays[hi]) for hi in range(group)]
        for hi in range(group):
            outputs(r0, hi, *staged[hi])
        return 0

    lax.fori_loop(0, t // chunk, body, 0)
    for hi in range(group):
        s_ref[hi] = st_ref[hi].T


def _hg_prompt(proj, lbp, out_gain, layer, hg_index, batch):
    m, d4 = proj.shape
    d = d4 // 4
    heads = d // HEAD_DIM
    t = m // batch
    chunk = _pick(t, HG_CHUNK, HG_SUB)
    sub = HG_SUB
    group = HG_HEADS_PER_STEP if heads % HG_HEADS_PER_STEP == 0 else 1
    width = group * HEAD_DIM
    ng = heads // group
    col = lambda sec: (lambda b, h: (b, sec * ng + h))
    return pl.pallas_call(
        functools.partial(_hg_prompt_kernel, chunk=chunk, sub=sub, group=group),
        out_shape=(jax.ShapeDtypeStruct((m, d), BF16),
                   jax.ShapeDtypeStruct((batch, heads, HEAD_DIM, HEAD_DIM), F32)),
        grid=(batch, ng),
        in_specs=[pl.BlockSpec((t, width), col(0)),
                  pl.BlockSpec((t, width), col(1)),
                  pl.BlockSpec((t, width), col(2)),
                  pl.BlockSpec((t, width), col(3)),
                  pl.BlockSpec((None, SUBLANES, width), lambda b, h: (layer, 0, h)),
                  pl.BlockSpec((None, 1, HEAD_DIM), lambda b, h: (hg_index, 0, 0))],
        out_specs=(pl.BlockSpec((t, width), lambda b, h: (b, h)),
                   pl.BlockSpec((None, group, HEAD_DIM, HEAD_DIM), lambda b, h: (b, h, 0, 0))),
        scratch_shapes=[pltpu.VMEM((group, HEAD_DIM, HEAD_DIM), F32),
                        pltpu.VMEM((group, chunk, HEAD_DIM), F32)],
        compiler_params=_params("arbitrary", "arbitrary"),
        name="hg_prompt",
    )(proj, proj, proj, proj, lbp, out_gain[:, None, :])


def _hg_sample_kernel(q_ref, f_ref, i_ref, g_ref, lbp_ref, gain_ref, s0_ref, o_ref, s_ref, pad_ref):
    tq = q_ref.shape[0]
    glog, kk = _hg_gates(f_ref[...], lbp_ref[...])

    def columns(x):
        pad_ref[...] = jnp.zeros_like(pad_ref)
        pad_ref[0:tq, :] = x
        return pad_ref[...].T

    q_c = columns(q_ref[...])
    f_c = columns(jnp.exp(glog))
    k_c = columns(kk)
    v = i_ref[...]
    s = s0_ref[...]
    for ti in range(tq):
        s = f_c[:, ti:ti + 1] * s + k_c[:, ti:ti + 1] * v[ti:ti + 1, :]
        o = jnp.sum(q_c[:, ti:ti + 1] * s, axis=0, keepdims=True)
        o_ref[ti:ti + 1, :] = _hg_output(o, g_ref[ti:ti + 1, :], gain_ref[...])
    s_ref[...] = s


def _hg_sample(proj, lbp, out_gain, state, layer, hg_index, batch):
    m, d4 = proj.shape
    d = d4 // 4
    heads = d // HEAD_DIM
    tq = m // batch
    proj3 = proj.reshape(batch, tq, d4)
    col = lambda sec: (lambda b, h: (b, 0, sec * heads + h))
    out, s_new = pl.pallas_call(
        _hg_sample_kernel,
        out_shape=(jax.ShapeDtypeStruct((batch, tq, d), F32),
                   jax.ShapeDtypeStruct((batch, heads, HEAD_DIM, HEAD_DIM), F32)),
        grid=(batch, heads),
        in_specs=[pl.BlockSpec((None, tq, HEAD_DIM), col(0)),
                  pl.BlockSpec((None, tq, HEAD_DIM), col(1)),
                  pl.BlockSpec((None, tq, HEAD_DIM), col(2)),
                  pl.BlockSpec((None, tq, HEAD_DIM), col(3)),
                  pl.BlockSpec((None, SUBLANES, HEAD_DIM), lambda b, h: (layer, 0, h)),
                  pl.BlockSpec((None, 1, HEAD_DIM), lambda b, h: (hg_index, 0, 0)),
                  pl.BlockSpec((None, None, None, HEAD_DIM, HEAD_DIM), lambda b, h: (hg_index, b, h, 0, 0))],
        out_specs=(pl.BlockSpec((None, tq, HEAD_DIM), lambda b, h: (b, 0, h)),
                   pl.BlockSpec((None, None, HEAD_DIM, HEAD_DIM), lambda b, h: (b, h, 0, 0))),
        scratch_shapes=[pltpu.VMEM((HEAD_DIM, HEAD_DIM), F32)],
        compiler_params=_params("arbitrary", "arbitrary"),
        name="hg_sample",
    )(proj3, proj3, proj3, proj3, lbp, out_gain[:, None, :], state)
    return out.reshape(m, d), s_new


def _ffn_gate_kernel(a_ref, b_ref, buf_ref, w_ref, cb_ref, o_ref, st_ref, ext_ref):
    t = a_ref.shape[0]
    lead = SUBLANES - (FFN_CONV_W - 1)
    ext_ref[lead:SUBLANES, :] = buf_ref[...]
    ext_ref[SUBLANES:SUBLANES + t, :] = a_ref[...]
    c = cb_ref[...]
    for j in range(FFN_CONV_W):
        c = c + w_ref[j:j + 1, :] * ext_ref[lead + j:lead + j + t, :]
    o_ref[...] = (c * (1.0 / (1.0 + jnp.exp(-c))) * b_ref[...]).astype(o_ref.dtype)
    st_ref[...] = ext_ref[SUBLANES + t - (FFN_CONV_W - 1):SUBLANES + t, :]


def _ffn_gate(ab, buf, conv_w, conv_b, layer, buf_layer, batch, out_dtype):
    m, f2 = ab.shape
    f = f2 // 2
    t = m // batch
    tf = _pick(f, 512)
    nf = f // tf
    ab3 = ab.reshape(batch, t, f2)
    out, st = pl.pallas_call(
        _ffn_gate_kernel,
        out_shape=(jax.ShapeDtypeStruct((batch, t, f), out_dtype),
                   jax.ShapeDtypeStruct((batch, FFN_CONV_W - 1, f), F32)),
        grid=(batch, nf),
        in_specs=[pl.BlockSpec((None, t, tf), lambda b, j: (b, 0, j)),
                  pl.BlockSpec((None, t, tf), lambda b, j: (b, 0, nf + j)),
                  pl.BlockSpec((None, None, FFN_CONV_W - 1, tf), lambda b, j: (buf_layer, b, 0, j)),
                  pl.BlockSpec((None, FFN_CONV_W, tf), lambda b, j: (layer, 0, j)),
                  pl.BlockSpec((None, 1, tf), lambda b, j: (layer, 0, j))],
        out_specs=(pl.BlockSpec((None, t, tf), lambda b, j: (b, 0, j)),
                   pl.BlockSpec((None, FFN_CONV_W - 1, tf), lambda b, j: (b, 0, j))),
        scratch_shapes=[pltpu.VMEM((t + SUBLANES, tf), F32)],
        compiler_params=_params("arbitrary", "arbitrary"),
        name="ffn_gate",
    )(ab3, ab3, buf, conv_w, conv_b[:, None, :])
    return out.reshape(m, f), st


def kernel(x_prompt, x_sample, cache_sb_k, cache_sb_v, page_table, state_hgrn, state_ffn_conv,
           norm_mixer, norm_ffn, w_sb_qkv, sb_q_gain, sb_k_gain, sb_logit_bias, w_sb_o,
           w_hg_in, hg_lower_bounds, hg_out_gain, w_hg_o,
           w_ffn_in, ffn_conv_w, ffn_conv_b, w_ffn_out):
    bp, tp, d = x_prompt.shape
    bs, ts, _ = x_sample.shape
    depth = norm_mixer.shape[0]
    heads = d // HEAD_DIM
    n_mixers = 2
    f = w_ffn_in.shape[-1] // 2
    n_sb, pool, page = cache_sb_k.shape[:3]
    cache_k = cache_sb_k.reshape(n_sb, pool, page * heads, HEAD_DIM)
    cache_v = cache_sb_v.reshape(n_sb, pool, page * heads, HEAD_DIM)
    zero_buf = jnp.zeros((1, bp, FFN_CONV_W - 1, f), F32)
    lbp = _lower_bounds(hg_lower_bounds)

    xp = x_prompt.reshape(bp * tp, d)
    xs = x_sample.reshape(bs * ts, d)
    kp_l, vp_l, ks_l, vs_l, sp_l, ss_l, cp_l, cs_l = [], [], [], [], [], [], [], []
    for layer in range(depth):
        j = layer // n_mixers
        hp = _rmsnorm(xp, norm_mixer, layer)
        hs = _rmsnorm(xs, norm_mixer, layer)
        if layer % n_mixers == 0:
            qkv_p = _matmul(hp, w_sb_qkv, j, sections=3)
            qkv_s = _matmul(hs, w_sb_qkv, j, sections=3)
            kb_p, vb_p, knh_p, vh_p = _kv_heads(qkv_p, sb_k_gain, j)
            _, _, knh_s, vh_s = _kv_heads(qkv_s, sb_k_gain, j)
            op = _sb_prompt(qkv_p, kb_p, vb_p, sb_q_gain, sb_logit_bias, j, bp)
            os_ = _sb_sample(qkv_s, knh_s, vh_s, sb_q_gain, sb_logit_bias, cache_k, cache_v,
                             page_table, j, bs)
            xp = _matmul(op, w_sb_o, j, res=xp)[0]
            xs = _matmul(os_, w_sb_o, j, res=xs)[0]
            kp_l.append(knh_p.reshape(bp, tp, heads, HEAD_DIM))
            vp_l.append(vh_p.reshape(bp, tp, heads, HEAD_DIM))
            ks_l.append(knh_s.reshape(bs, ts, heads, HEAD_DIM))
            vs_l.append(vh_s.reshape(bs, ts, heads, HEAD_DIM))
        else:
            proj_p = _matmul(hp, w_hg_in, j)[0]
            proj_s = _matmul(hs, w_hg_in, j)[0]
            op, sp = _hg_prompt(proj_p, lbp, hg_out_gain, layer, j, bp)
            os_, ss = _hg_sample(proj_s, lbp, hg_out_gain, state_hgrn, layer, j, bs)
            xp = _matmul(op, w_hg_o, j, res=xp)[0]
            xs = _matmul(os_, w_hg_o, j, res=xs)[0]
            sp_l.append(sp)
            ss_l.append(ss)
        hp = _rmsnorm(xp, norm_ffn, layer)
        hs = _rmsnorm(xs, norm_ffn, layer)
        ab_p = _matmul(hp, w_ffn_in, layer)[0]
        ab_s = _matmul(hs, w_ffn_in, layer)[0]
        gp, cp = _ffn_gate(ab_p, zero_buf, ffn_conv_w, ffn_conv_b, layer, 0, bp, BF16)
        gs, cs = _ffn_gate(ab_s, state_ffn_conv, ffn_conv_w, ffn_conv_b, layer, layer, bs, F32)
        xp = _matmul(gp, w_ffn_out, layer, res=xp, tm_target=512, tn_target=512)[0]
        xs = _matmul(gs, w_ffn_out, layer, res=xs, tn_target=512)[0]
        cp_l.append(cp)
        cs_l.append(cs)

    return (xp.reshape(bp, tp, d), xs.reshape(bs, ts, d),
            jnp.stack(kp_l), jnp.stack(vp_l), jnp.stack(ks_l), jnp.stack(vs_l),
            jnp.stack(sp_l), jnp.stack(ss_l), jnp.stack(cp_l), jnp.stack(cs_l))
```

```python
import functools

import jax
import jax.numpy as jnp
import numpy as np
from jax import lax
from jax.experimental import pallas as pl
from jax.experimental.pallas import tpu as pltpu

F32 = jnp.float32
BF16 = jnp.bfloat16

HEAD_DIM = 128
SUBLANES = 8
NORM_EPS = 1e-6
FFN_CONV_W = 3
SB_TILE = 256
SB_HEADS_PER_STEP = 4
SB_PAGES_PER_STEP = 2
FFN_ROW_BLOCKS = 8
HG_HEADS_PER_STEP = 4
HG_CHUNK = 64
HG_SUB = 16
VMEM_LIMIT_BYTES = 56 * 1024 * 1024


def _params(*semantics):
    return pltpu.CompilerParams(dimension_semantics=semantics, vmem_limit_bytes=VMEM_LIMIT_BYTES)


def _pick(n, target, quantum=HEAD_DIM):
    best = None
    for d in range(quantum, min(n, target) + 1, quantum):
        if n % d == 0:
            best = d
    return best if best is not None else n


def _softplus(z):
    return jnp.maximum(z, 0.0) + jnp.log(1.0 + jnp.exp(-jnp.abs(z)))


def _split_bf16(x, parts):
    out = []
    r = x
    for _ in range(parts - 1):
        p = r.astype(BF16)
        out.append(p)
        r = r - p.astype(F32)
    out.append(r.astype(BF16))
    return out


def _rmsnorm_kernel(x_ref, g_ref, o_ref):
    x = x_ref[...]
    ms = jnp.mean(x * x, axis=-1, keepdims=True)
    o_ref[...] = (x * lax.rsqrt(ms + NORM_EPS) * g_ref[...]).astype(o_ref.dtype)


def _rmsnorm(x, gains, layer):
    m, d = x.shape
    tm = _pick(m, 512, SUBLANES)
    return pl.pallas_call(
        _rmsnorm_kernel,
        out_shape=jax.ShapeDtypeStruct((m, d), BF16),
        grid=(m // tm,),
        in_specs=[pl.BlockSpec((tm, d), lambda i: (i, 0)),
                  pl.BlockSpec((None, 1, d), lambda i: (layer, 0, 0))],
        out_specs=pl.BlockSpec((tm, d), lambda i: (i, 0)),
        compiler_params=_params("arbitrary"),
        name="rmsnorm",
    )(x, gains[:, None, :])


def _matmul_kernel(*refs, has_res):
    if has_res:
        x_ref, w_ref, r_ref, xs_ref, rs_ref, o_ref, os_ref, wb_ref = refs
    else:
        x_ref, w_ref, xs_ref, o_ref, os_ref, wb_ref = refs

    @pl.when(pl.program_id(1) == 0)
    def _():
        wb_ref[...] = w_ref[...].astype(BF16)
        acc_s = jnp.dot(xs_ref[...].astype(BF16), wb_ref[...], preferred_element_type=F32)
        if has_res:
            acc_s = acc_s + rs_ref[...]
        os_ref[...] = acc_s

    acc = jnp.dot(x_ref[...].astype(BF16), wb_ref[...], preferred_element_type=F32)
    if has_res:
        acc = acc + r_ref[...]
    o_ref[...] = acc.astype(o_ref.dtype)


def _matmul(x, xs, w, layer, *, res=None, ress=None, sections=1, tm_target=1024, tn_target=1024):
    m, k = x.shape
    ms = xs.shape[0]
    n = w.shape[-1]
    ns = n // sections
    tm = _pick(m, tm_target, SUBLANES)
    tn = _pick(ns, tn_target)
    per = ns // tn
    in_specs = [pl.BlockSpec((tm, k), lambda j, i: (i, 0)),
                pl.BlockSpec((None, k, tn), lambda j, i: (layer, 0, j))]
    args = [x, w]
    if res is not None:
        assert sections == 1
        in_specs.append(pl.BlockSpec((tm, tn), lambda j, i: (i, j)))
        args.append(res)
    in_specs.append(pl.BlockSpec((ms, k), lambda j, i: (0, 0)))
    args.append(xs)
    if res is not None:
        in_specs.append(pl.BlockSpec((ms, tn), lambda j, i: (0, j)))
        args.append(ress)
    return pl.pallas_call(
        functools.partial(_matmul_kernel, has_res=res is not None),
        out_shape=(jax.ShapeDtypeStruct((sections, m, ns), F32),
                   jax.ShapeDtypeStruct((sections, ms, ns), F32)),
        grid=(n // tn, m // tm),
        in_specs=in_specs,
        out_specs=(pl.BlockSpec((None, tm, tn), lambda j, i: (j // per, i, j % per)),
                   pl.BlockSpec((None, ms, tn), lambda j, i: (j // per, 0, j % per))),
        scratch_shapes=[pltpu.VMEM((k, tn), BF16)],
        compiler_params=_params("arbitrary", "arbitrary"),
        name="matmul",
    )(*args)


def _kv_heads_kernel(k_ref, v_ref, g_ref, *rest, heads):
    kb_ref, vb_ref, knh_ref, vh_ref = rest[-4:]
    g = g_ref[...]
    tm = k_ref.shape[0]
    for h in range(heads):
        cols = slice(h * HEAD_DIM, (h + 1) * HEAD_DIM)
        x = k_ref[:, cols]
        ms = jnp.mean(x * x, axis=-1, keepdims=True)
        kn = x * lax.rsqrt(ms + NORM_EPS) * g
        v = v_ref[:, cols]
        kb_ref[:, cols] = kn.astype(BF16)
        vb_ref[:, cols] = v.astype(BF16)
        knh_ref[pl.ds(h, tm, stride=heads), :] = kn
        vh_ref[pl.ds(h, tm, stride=heads), :] = v


def _kv_heads(qkv3, gains, layer, n_layers, stacks):
    _, m, d = qkv3.shape
    heads = d // HEAD_DIM
    tm = _pick(m, 256, SUBLANES)
    flat = jax.ShapeDtypeStruct((m, d), BF16)
    by_head = jax.ShapeDtypeStruct((n_layers, m * heads, HEAD_DIM), F32)
    flat_spec = pl.BlockSpec((tm, d), lambda i: (i, 0))
    head_spec = pl.BlockSpec((None, tm * heads, HEAD_DIM), lambda i: (layer, i, 0))
    in_specs = [pl.BlockSpec((None, tm, d), lambda i: (1, i, 0)),
                pl.BlockSpec((None, tm, d), lambda i: (2, i, 0)),
                pl.BlockSpec((None, 1, HEAD_DIM), lambda i: (layer, 0, 0))]
    args = [qkv3, qkv3, gains[:, None, :]]
    aliases = {}
    if stacks is not None:
        in_specs += [pl.BlockSpec(memory_space=pl.ANY)] * 2
        args += list(stacks)
        aliases = {3: 2, 4: 3}
    return pl.pallas_call(
        functools.partial(_kv_heads_kernel, heads=heads),
        out_shape=(flat, flat, by_head, by_head),
        grid=(m // tm,),
        in_specs=in_specs,
        out_specs=(flat_spec, flat_spec, head_spec, head_spec),
        input_output_aliases=aliases,
        compiler_params=_params("arbitrary"),
        name="kv_heads",
    )(*args)


def _suffix_ones(n):
    r = lax.broadcasted_iota(jnp.int32, (n, n), 0)
    c = lax.broadcasted_iota(jnp.int32, (n, n), 1)
    return jnp.where(r > c, 1.0, 0.0).astype(BF16)


def _nt_dot(a, b):
    return lax.dot_general(a, b, (((1,), (1,)), ((), ())), preferred_element_type=F32)


def _sb_log_keep(z, mask):
    sp = _softplus(z)
    lk = -sp if mask is None else jnp.where(mask, -sp, 0.0)
    return lk, z - sp


def _sb_tail(lk, carry, u):
    tail = carry
    for part in _split_bf16(lk, 2):
        tail = tail + jnp.dot(part, u, preferred_element_type=F32)
    return tail


def _sb_weights(z, carry, u, mask):
    lk, log_beta = _sb_log_keep(z, mask)
    a = jnp.exp(log_beta + _sb_tail(lk, carry, u))
    if mask is not None:
        a = jnp.where(mask, a, 0.0)
    return a, carry + jnp.sum(lk, axis=-1, keepdims=True)


def _normed_query(q, gain):
    ms = jnp.mean(q * q, axis=-1, keepdims=True)
    return q * lax.rsqrt(ms + NORM_EPS) * gain * (HEAD_DIM ** -0.5)


def _sb_prompt_kernel(bias_ref, q_ref, k_ref, v_ref, qg_ref, o_ref, qb_ref, acc_ref, carry_ref,
                      *, group):
    hg = pl.program_id(1)
    i = pl.program_id(2)
    tile = q_ref.shape[0]
    u = _suffix_ones(tile)
    row = lax.broadcasted_iota(jnp.int32, (tile, tile), 0)
    col = lax.broadcasted_iota(jnp.int32, (tile, tile), 1)
    for g in range(group):
        cols = slice(g * HEAD_DIM, (g + 1) * HEAD_DIM)
        qb_ref[:, cols] = _normed_query(q_ref[:, cols], qg_ref[...]).astype(BF16)
    acc_ref[...] = jnp.zeros_like(acc_ref)
    carry_ref[...] = jnp.zeros_like(carry_ref)

    def key_tile(j, mask):
        keys = pl.ds(pl.multiple_of(j * tile, tile), tile)
        heads = [slice(g * HEAD_DIM, (g + 1) * HEAD_DIM) for g in range(group)]
        zs = [_nt_dot(qb_ref[:, cols], k_ref[keys, cols]) + bias_ref[hg * group + g]
              for g, cols in enumerate(heads)]
        keeps = [_sb_log_keep(z, mask) for z in zs]
        tails = [_sb_tail(lk, carry_ref[g], u) for g, (lk, _) in enumerate(keeps)]
        for g, cols in enumerate(heads):
            lk, log_beta = keeps[g]
            a = jnp.exp(log_beta + tails[g])
            if mask is not None:
                a = jnp.where(mask, a, 0.0)
            acc_ref[:, cols] += jnp.dot(a.astype(BF16), v_ref[keys, cols],
                                        preferred_element_type=F32)
            carry_ref[g] += jnp.sum(lk, axis=-1, keepdims=True)

    key_tile(i, col < row)

    def body(step, _):
        key_tile(i - 1 - step, None)
        return 0

    lax.fori_loop(0, i, body, 0)
    o_ref[...] = acc_ref[...].astype(o_ref.dtype)


def _sb_prompt(qkv3, kb, vb, q_gain, bias, layer, batch):
    _, m, d = qkv3.shape
    t = m // batch
    heads = d // HEAD_DIM
    tile = _pick(t, SB_TILE)
    nq = t // tile
    group = SB_HEADS_PER_STEP if heads % SB_HEADS_PER_STEP == 0 else 1
    width = group * HEAD_DIM
    kv_spec = pl.BlockSpec((t, width), lambda b, h, i: (b, h))
    return pl.pallas_call(
        functools.partial(_sb_prompt_kernel, group=group),
        out_shape=jax.ShapeDtypeStruct((m, d), BF16),
        grid=(batch, heads // group, nq),
        in_specs=[pl.BlockSpec(memory_space=pltpu.SMEM),
                  pl.BlockSpec((None, tile, width), lambda b, h, i: (0, b * nq + i, h)),
                  kv_spec,
                  kv_spec,
                  pl.BlockSpec((None, 1, HEAD_DIM), lambda b, h, i: (layer, 0, 0))],
        out_specs=pl.BlockSpec((tile, width), lambda b, h, i: (b * nq + i, h)),
        scratch_shapes=[pltpu.VMEM((tile, width), BF16),
                        pltpu.VMEM((tile, width), F32),
                        pltpu.VMEM((group, tile, 1), F32)],
        compiler_params=_params("arbitrary", "arbitrary", "arbitrary"),
        name="sb_prompt",
    )(bias[layer], qkv3, kb, vb, q_gain[:, None, :])


def _sb_sample_kernel(pt_ref, q_ref, kn_ref, vn_ref, qg_ref, bias_ref, hmask_ref, wsuf_ref, *rest,
                      heads, tq, page, pages_per_step):
    del pt_ref
    kc_refs = rest[:pages_per_step]
    vc_refs = rest[pages_per_step:2 * pages_per_step]
    o_ref, qall_ref, acc_ref, carry_ref, kpad_ref, vpad_ref = rest[2 * pages_per_step:]
    p = pl.program_id(1)
    lanes = page * heads
    n_blk = lanes // HEAD_DIM
    n_grp = heads * tq // SUBLANES

    def tiles(kv_refs, valid):
        logits = []
        for k_ref, _ in kv_refs:
            f = _nt_dot(qall_ref[...].astype(BF16), k_ref[...].astype(BF16)) * hmask_ref[...]
            zs = f[0:SUBLANES]
            for g in range(1, n_grp):
                zs = zs + f[g * SUBLANES:(g + 1) * SUBLANES]
            z = zs
            for k in range(1, SUBLANES // tq):
                z = z + pltpu.roll(zs, k * tq, axis=0)
            z = z + jnp.concatenate([bias_ref[...]] * n_blk, axis=1)
            logits.append(_sb_log_keep(z, valid))
        sums = []
        for lk, _ in logits:
            blocks = jnp.concatenate(
                [lk[:, b * HEAD_DIM:(b + 1) * HEAD_DIM] for b in range(n_blk)], axis=0)
            res = jnp.zeros((n_blk * SUBLANES, 2 * HEAD_DIM), F32)
            for part in _split_bf16(blocks, 2):
                res = res + jnp.dot(part, wsuf_ref[...], preferred_element_type=F32)
            sums.append(res)
        run = carry_ref[...]
        out = jnp.zeros(acc_ref.shape, F32)
        for (_, log_beta), res, (_, v_ref) in zip(logits, sums, kv_refs):
            tails = [None] * n_blk
            for b in reversed(range(n_blk)):
                blk = res[b * SUBLANES:(b + 1) * SUBLANES]
                tails[b] = blk[:, :HEAD_DIM] + run
                run = run + blk[:, HEAD_DIM:]
            a = jnp.exp(log_beta + jnp.concatenate(tails, axis=1))
            if valid is not None:
                a = jnp.where(valid, a, 0.0)
            a_heads = (jnp.concatenate([a] * n_grp, axis=0) * hmask_ref[...]).astype(BF16)
            out = out + jnp.dot(a_heads, v_ref[...].astype(BF16), preferred_element_type=F32)
        carry_ref[...] = run
        acc_ref[...] += out

    @pl.when(p == 0)
    def _new_tokens():
        for h in range(heads):
            cols = slice(h * HEAD_DIM, (h + 1) * HEAD_DIM)
            qall_ref[h * tq:(h + 1) * tq, :] = _normed_query(q_ref[:, cols], qg_ref[...])
        kpad_ref[...] = jnp.zeros_like(kpad_ref)
        vpad_ref[...] = jnp.zeros_like(vpad_ref)
        kpad_ref[0:tq * heads, :] = kn_ref[...]
        vpad_ref[0:tq * heads, :] = vn_ref[...]
        acc_ref[...] = jnp.zeros_like(acc_ref)
        carry_ref[...] = jnp.zeros_like(carry_ref)
        query = lax.broadcasted_iota(jnp.int32, (SUBLANES, lanes), 0) % tq
        position = lax.broadcasted_iota(jnp.int32, (SUBLANES, lanes), 1) // heads
        tiles([(kpad_ref, vpad_ref)], position < query)

    @pl.when(p > 0)
    def _past_page():
        tiles(list(zip(kc_refs, vc_refs)), None)

    @pl.when(p == pl.num_programs(1) - 1)
    def _finish():
        for h in range(heads):
            o_ref[:, h * HEAD_DIM:(h + 1) * HEAD_DIM] = acc_ref[h * tq:(h + 1) * tq, :]


def _sb_sample(qkv3, kn, vh, q_gain, bias, cache_k, cache_v, page_table, layer, batch):
    _, m, d = qkv3.shape
    tq = m // batch
    heads = d // HEAD_DIM
    assert SUBLANES % tq == 0 and (heads * tq) % SUBLANES == 0 and HEAD_DIM % heads == 0
    rows = heads * tq
    n_pages = page_table.shape[1]
    lanes = cache_k.shape[2]
    page = lanes // heads
    qkv4 = qkv3.reshape(3, batch, tq, d)
    lane_head = np.arange(lanes) % heads
    bias_tile = jnp.broadcast_to(jnp.tile(bias[layer], HEAD_DIM // heads)[None, :],
                                 (SUBLANES, HEAD_DIM)).astype(F32)
    hmask = jnp.asarray(lane_head[None, :] == (np.arange(rows) // tq)[:, None], F32)
    l = np.arange(HEAD_DIM)
    same_head = (l[:, None] % heads) == (l[None, :] % heads)
    later = (l[:, None] // heads) > (l[None, :] // heads)
    wsuf = jnp.asarray(np.concatenate([same_head & later, same_head], axis=1), BF16)

    pps = SB_PAGES_PER_STEP if n_pages % SB_PAGES_PER_STEP == 0 else 1

    def page_index(j):
        return lambda b, p, pt: (layer, pt[b, n_pages - 1 - ((jnp.maximum(p, 1) - 1) * pps + j)], 0, 0)

    page_specs = [pl.BlockSpec((None, None, lanes, HEAD_DIM), page_index(j)) for j in range(pps)]
    fixed = lambda shape: pl.BlockSpec(shape, lambda b, p, pt: (0,) * len(shape))
    new_spec = pl.BlockSpec((None, None, tq * heads, HEAD_DIM), lambda b, p, pt: (layer, b, 0, 0))
    stack_shape = (kn.shape[0], batch, tq * heads, HEAD_DIM)
    grid_spec = pltpu.PrefetchScalarGridSpec(
        num_scalar_prefetch=1,
        grid=(batch, n_pages // pps + 1),
        in_specs=[pl.BlockSpec((None, None, tq, d), lambda b, p, pt: (0, b, 0, 0)),
                  new_spec,
                  new_spec,
                  pl.BlockSpec((None, 1, HEAD_DIM), lambda b, p, pt: (layer, 0, 0)),
                  fixed(bias_tile.shape),
                  fixed(hmask.shape),
                  fixed(wsuf.shape)] + page_specs + page_specs,
        out_specs=pl.BlockSpec((None, tq, d), lambda b, p, pt: (b, 0, 0)),
        scratch_shapes=[pltpu.VMEM((rows, HEAD_DIM), F32),
                        pltpu.VMEM((rows, HEAD_DIM), F32),
                        pltpu.VMEM((SUBLANES, HEAD_DIM), F32),
                        pltpu.VMEM((lanes, HEAD_DIM), F32),
                        pltpu.VMEM((lanes, HEAD_DIM), F32)])
    out = pl.pallas_call(
        functools.partial(_sb_sample_kernel, heads=heads, tq=tq, page=page, pages_per_step=pps),
        out_shape=jax.ShapeDtypeStruct((batch, tq, d), F32),
        grid_spec=grid_spec,
        compiler_params=_params("arbitrary", "arbitrary"),
        name="sb_sample",
    )(page_table, qkv4, kn.reshape(stack_shape), vh.reshape(stack_shape),
      q_gain[:, None, :], bias_tile, hmask, wsuf, *([cache_k] * pps), *([cache_v] * pps))
    return out.reshape(m, d)


def _lower_bounds_kernel(x_ref, lb_ref):
    x = x_ref[...]
    e = jnp.exp(x - jnp.max(x, axis=0, keepdims=True))
    soft = e / jnp.sum(e, axis=0, keepdims=True)
    depth = x.shape[0]
    run = jnp.zeros_like(soft[0:1])
    for layer in range(depth):
        run = run + soft[layer:layer + 1]
        lb = run - soft[0:1]
        lb_ref[layer, 0:1, :] = lb
        lb_ref[layer, 1:2, :] = jnp.log(lb)
        lb_ref[layer, 2:3, :] = jnp.log1p(-lb)
        lb_ref[layer, 3:4, :] = 1.0 - lb
        lb_ref[layer, 4:8, :] = jnp.zeros((4, x.shape[1]), F32)


def _lower_bounds(hg_lower_bounds):
    depth, d = hg_lower_bounds.shape
    return pl.pallas_call(
        _lower_bounds_kernel,
        out_shape=jax.ShapeDtypeStruct((depth, SUBLANES, d), F32),
        name="hg_lower_bounds",
    )(hg_lower_bounds.astype(F32))


def _hg_gates(f, lbp):
    log_lb, log_1m, one_m = lbp[1:2], lbp[2:3], lbp[3:4]
    log_sig = jnp.minimum(f, 0.0) - jnp.log1p(jnp.exp(-jnp.abs(f)))
    b = log_1m + log_sig
    g = jnp.maximum(log_lb, b) + jnp.log1p(jnp.exp(-jnp.abs(log_lb - b)))
    key = one_m * (1.0 / (1.0 + jnp.exp(f)))
    return g, key


def _hg_output(o, gate, gain):
    ms = jnp.mean(o * o, axis=-1, keepdims=True)
    on = o * lax.rsqrt(ms + NORM_EPS) * gain
    return on * (gate * (1.0 / (1.0 + jnp.exp(-gate))))


def _hg_prompt_kernel(q_ref, f_ref, i_ref, g_ref, lbp_ref, gain_ref, o_ref, s_ref,
                      st_ref, gc_ref, *, chunk, sub, group):
    t = q_ref.shape[0]
    n_sub = chunk // sub
    gain = gain_ref[...]
    r = lax.broadcasted_iota(jnp.int32, (chunk, chunk), 0)
    c = lax.broadcasted_iota(jnp.int32, (chunk, chunk), 1)
    lower = jnp.where(c <= r, 1.0, 0.0).astype(BF16)
    crow = lax.broadcasted_iota(jnp.int32, (chunk, HEAD_DIM), 0)
    srow = lax.broadcasted_iota(jnp.int32, (sub, HEAD_DIM), 0)
    own = (lax.broadcasted_iota(jnp.int32, (sub, sub * sub), 1) // sub
           == lax.broadcasted_iota(jnp.int32, (sub, sub * sub), 0))
    st_ref[...] = jnp.zeros_like(st_ref)

    def head_cols(hi):
        return slice(hi * HEAD_DIM, (hi + 1) * HEAD_DIM)

    def decay(r0, hi):
        glog, kk = _hg_gates(f_ref[pl.ds(r0, chunk), head_cols(hi)], lbp_ref[:, head_cols(hi)])
        gc = jnp.zeros((chunk, HEAD_DIM), F32)
        for part in _split_bf16(glog, 3):
            gc = gc + jnp.dot(lower, part, preferred_element_type=F32)
        return gc, kk

    def scores(r0, hi, gc, kk):
        rows = pl.ds(r0, chunk)
        gc_ref[hi] = gc
        q = q_ref[rows, head_cols(hi)]
        vb = i_ref[rows, head_cols(hi)].astype(BF16)
        st = st_ref[hi]
        inter = _nt_dot((q * jnp.exp(gc)).astype(BF16), st.astype(BF16))
        g_last = gc[chunk - 1:chunk, :]
        offs, diags = [], []
        for si in range(n_sub):
            base = si * sub
            blk = slice(base, base + sub)
            q_s, g_s, k_s = q[blk], gc[blk], kk[blk]
            if si > 0:
                g_start = gc[base - 1:base, :]
                k_prev = jnp.where(crow < base, kk * jnp.exp(jnp.minimum(g_start - gc, 0.0)), 0.0)
                q_in = q_s * jnp.exp(g_s - g_start)
                offs.append(_nt_dot(q_in.astype(BF16), k_prev.astype(BF16)))
            else:
                offs.append(None)
            stacked = []
            for ti in range(sub):
                g_t = gc_ref[hi, base + ti:base + ti + 1, :]
                k_t = jnp.where(srow <= ti, k_s * jnp.exp(jnp.minimum(g_t - g_s, 0.0)), 0.0)
                stacked.append(k_t.astype(BF16))
            diags.append(_nt_dot(q_s.astype(BF16), jnp.concatenate(stacked, axis=0)))
        k_dec = kk * jnp.exp(g_last - gc)
        st_ref[hi] = st * jnp.exp(g_last) + lax.dot_general(
            vb, k_dec.astype(BF16), (((0,), (0,)), ((), ())), preferred_element_type=F32)
        return vb, inter, offs, diags

    def outputs(r0, hi, vb, inter, offs, diags):
        for si in range(n_sub):
            base = si * sub
            blk = slice(base, base + sub)
            o_s = inter[blk]
            if offs[si] is not None:
                o_s = o_s + jnp.dot(offs[si].astype(BF16), vb, preferred_element_type=F32)
            sc = jnp.where(own, diags[si], 0.0).astype(BF16)
            o_s = o_s + jnp.dot(sc, jnp.concatenate([vb[blk]] * sub, axis=0),
                                preferred_element_type=F32)
            out_rows = pl.ds(pl.multiple_of(r0 + base, sub), sub)
            o_ref[out_rows, head_cols(hi)] = _hg_output(
                o_s, g_ref[out_rows, head_cols(hi)], gain).astype(o_ref.dtype)

    def body(ci, _):
        r0 = pl.multiple_of(ci * chunk, chunk)
        decays = [decay(r0, hi) for hi in range(group)]
        staged = [scores(r0, hi, *decays[hi]) for hi in range(group)]
        for hi in range(group):
            outputs(r0, hi, *staged[hi])
        return 0

    lax.fori_loop(0, t // chunk, body, 0)
    for hi in range(group):
        s_ref[hi] = st_ref[hi].T


def _hg_prompt(proj, lbp, out_gain, layer, hg_index, batch):
    m, d4 = proj.shape
    d = d4 // 4
    heads = d // HEAD_DIM
    t = m // batch
    chunk = _pick(t, HG_CHUNK, HG_SUB)
    sub = HG_SUB
    group = HG_HEADS_PER_STEP if heads % HG_HEADS_PER_STEP == 0 else 1
    width = group * HEAD_DIM
    ng = heads // group
    col = lambda sec: (lambda b, h: (b, sec * ng + h))
    return pl.pallas_call(
        functools.partial(_hg_prompt_kernel, chunk=chunk, sub=sub, group=group),
        out_shape=(jax.ShapeDtypeStruct((m, d), BF16),
                   jax.ShapeDtypeStruct((batch, heads, HEAD_DIM, HEAD_DIM), F32)),
        grid=(batch, ng),
        in_specs=[pl.BlockSpec((t, width), col(0)),
                  pl.BlockSpec((t, width), col(1)),
                  pl.BlockSpec((t, width), col(2)),
                  pl.BlockSpec((t, width), col(3)),
                  pl.BlockSpec((None, SUBLANES, width), lambda b, h: (layer, 0, h)),
                  pl.BlockSpec((None, 1, HEAD_DIM), lambda b, h: (hg_index, 0, 0))],
        out_specs=(pl.BlockSpec((t, width), lambda b, h: (b, h)),
                   pl.BlockSpec((None, group, HEAD_DIM, HEAD_DIM), lambda b, h: (b, h, 0, 0))),
        scratch_shapes=[pltpu.VMEM((group, HEAD_DIM, HEAD_DIM), F32),
                        pltpu.VMEM((group, chunk, HEAD_DIM), F32)],
        compiler_params=_params("arbitrary", "arbitrary"),
        name="hg_prompt",
    )(proj, proj, proj, proj, lbp, out_gain[:, None, :])


def _hg_sample_kernel(q_ref, f_ref, i_ref, g_ref, lbp_ref, gain_ref, s0_ref, o_ref, s_ref, pad_ref):
    tq = q_ref.shape[0]
    glog, kk = _hg_gates(f_ref[...], lbp_ref[...])

    def columns(x):
        pad_ref[...] = jnp.zeros_like(pad_ref)
        pad_ref[0:tq, :] = x
        return pad_ref[...].T

    q_c = columns(q_ref[...])
    f_c = columns(jnp.exp(glog))
    k_c = columns(kk)
    v = i_ref[...]
    s = s0_ref[...]
    for ti in range(tq):
        s = f_c[:, ti:ti + 1] * s + k_c[:, ti:ti + 1] * v[ti:ti + 1, :]
        o = jnp.sum(q_c[:, ti:ti + 1] * s, axis=0, keepdims=True)
        o_ref[ti:ti + 1, :] = _hg_output(o, g_ref[ti:ti + 1, :], gain_ref[...])
    s_ref[...] = s


def _hg_sample(proj, lbp, out_gain, state, layer, hg_index, batch):
    m, d4 = proj.shape
    d = d4 // 4
    heads = d // HEAD_DIM
    tq = m // batch
    proj3 = proj.reshape(batch, tq, d4)
    col = lambda sec: (lambda b, h: (b, 0, sec * heads + h))
    out, s_new = pl.pallas_call(
        _hg_sample_kernel,
        out_shape=(jax.ShapeDtypeStruct((batch, tq, d), F32),
                   jax.ShapeDtypeStruct((batch, heads, HEAD_DIM, HEAD_DIM), F32)),
        grid=(batch, heads),
        in_specs=[pl.BlockSpec((None, tq, HEAD_DIM), col(0)),
                  pl.BlockSpec((None, tq, HEAD_DIM), col(1)),
                  pl.BlockSpec((None, tq, HEAD_DIM), col(2)),
                  pl.BlockSpec((None, tq, HEAD_DIM), col(3)),
                  pl.BlockSpec((None, SUBLANES, HEAD_DIM), lambda b, h: (layer, 0, h)),
                  pl.BlockSpec((None, 1, HEAD_DIM), lambda b, h: (hg_index, 0, 0)),
                  pl.BlockSpec((None, None, None, HEAD_DIM, HEAD_DIM), lambda b, h: (hg_index, b, h, 0, 0))],
        out_specs=(pl.BlockSpec((None, tq, HEAD_DIM), lambda b, h: (b, 0, h)),
                   pl.BlockSpec((None, None, HEAD_DIM, HEAD_DIM), lambda b, h: (b, h, 0, 0))),
        scratch_shapes=[pltpu.VMEM((HEAD_DIM, HEAD_DIM), F32)],
        compiler_params=_params("arbitrary", "arbitrary"),
        name="hg_sample",
    )(proj3, proj3, proj3, proj3, lbp, out_gain[:, None, :], state)
    return out.reshape(m, d), s_new


def _ffn_gate_kernel(a_ref, b_ref, buf_ref, w_ref, cb_ref, o_ref, st_ref, ext_ref):
    t = a_ref.shape[0]
    ext_ref[CONV_LEAD:SUBLANES, :] = buf_ref[...]
    o_ref[...] = _conv_gate(ext_ref, a_ref[...], b_ref[...], w_ref, cb_ref).astype(o_ref.dtype)
    st_ref[...] = ext_ref[SUBLANES + t - (FFN_CONV_W - 1):SUBLANES + t, :]


CONV_LEAD = SUBLANES - (FFN_CONV_W - 1)


def _conv_gate(ext_ref, a, b, w_ref, cb_ref):
    t = a.shape[0]
    lead = CONV_LEAD
    ext_ref[SUBLANES:SUBLANES + t, :] = a
    c = cb_ref[...]
    for j in range(FFN_CONV_W - 1):
        c = c + w_ref[j:j + 1, :] * ext_ref[lead + j:lead + j + t, :]
    c = c + w_ref[FFN_CONV_W - 1:FFN_CONV_W, :] * a
    return c * (1.0 / (1.0 + jnp.exp(-c))) * b


def _ffn_in_kernel(x_ref, wa_ref, wb_ref, xs_ref, cw_ref, cb_ref, o_ref, st_ref, as_ref, bs_ref,
                   wa16_ref, wb16_ref, ext_ref, *, tiles_per_seq, row_blocks):
    i = pl.program_id(1)
    tm = x_ref.shape[0]

    @pl.when(i == 0)
    def _():
        wa16_ref[...] = wa_ref[...].astype(BF16)
        wb16_ref[...] = wb_ref[...].astype(BF16)
        xs = xs_ref[...].astype(BF16)
        as_ref[...] = jnp.dot(xs, wa16_ref[...], preferred_element_type=F32)
        bs_ref[...] = jnp.dot(xs, wb16_ref[...], preferred_element_type=F32)

    @pl.when(i % tiles_per_seq == 0)
    def _():
        ext_ref[CONV_LEAD:SUBLANES, :] = jnp.zeros((FFN_CONV_W - 1, ext_ref.shape[1]), F32)

    @pl.when(i % tiles_per_seq != 0)
    def _():
        ext_ref[CONV_LEAD:SUBLANES, :] = ext_ref[SUBLANES + tm - (FFN_CONV_W - 1):SUBLANES + tm, :]

    rb = tm // row_blocks

    def products(r):
        x = x_ref[r * rb:(r + 1) * rb, :]
        a = jnp.dot(x, wa16_ref[...], preferred_element_type=F32)
        ext_ref[SUBLANES + r * rb:SUBLANES + (r + 1) * rb, :] = a
        return a, jnp.dot(x, wb16_ref[...], preferred_element_type=F32)

    def gate(r, a, b):
        lo = CONV_LEAD + r * rb
        c = cb_ref[...]
        for j in range(FFN_CONV_W - 1):
            c = c + cw_ref[j:j + 1, :] * ext_ref[lo + j:lo + j + rb, :]
        c = c + cw_ref[FFN_CONV_W - 1:FFN_CONV_W, :] * a
        o_ref[r * rb:(r + 1) * rb, :] = (c * (1.0 / (1.0 + jnp.exp(-c))) * b).astype(o_ref.dtype)

    pending = products(0)
    for r in range(1, row_blocks):
        issued = products(r)
        gate(r - 1, *pending)
        pending = issued
    gate(row_blocks - 1, *pending)
    st_ref[...] = ext_ref[SUBLANES + tm - (FFN_CONV_W - 1):SUBLANES + tm, :]


def _ffn_in(x, xs, w_in, conv_w, conv_b, layer, batch):
    m, k = x.shape
    ms = xs.shape[0]
    f = w_in.shape[-1] // 2
    t = m // batch
    tm = _pick(t, 1024, SUBLANES)
    tps = t // tm
    tf = _pick(f, 512)
    nf = f // tf
    return pl.pallas_call(
        functools.partial(_ffn_in_kernel, tiles_per_seq=tps,
                          row_blocks=FFN_ROW_BLOCKS if tm % (FFN_ROW_BLOCKS * SUBLANES) == 0 else 1),
        out_shape=(jax.ShapeDtypeStruct((m, f), BF16),
                   jax.ShapeDtypeStruct((batch, FFN_CONV_W - 1, f), F32),
                   jax.ShapeDtypeStruct((ms, f), F32),
                   jax.ShapeDtypeStruct((ms, f), F32)),
        grid=(nf, m // tm),
        in_specs=[pl.BlockSpec((tm, k), lambda j, i: (i, 0)),
                  pl.BlockSpec((None, k, tf), lambda j, i: (layer, 0, j)),
                  pl.BlockSpec((None, k, tf), lambda j, i: (layer, 0, nf + j)),
                  pl.BlockSpec((ms, k), lambda j, i: (0, 0)),
                  pl.BlockSpec((None, FFN_CONV_W, tf), lambda j, i: (layer, 0, j)),
                  pl.BlockSpec((None, 1, tf), lambda j, i: (layer, 0, j))],
        out_specs=(pl.BlockSpec((tm, tf), lambda j, i: (i, j)),
                   pl.BlockSpec((None, FFN_CONV_W - 1, tf), lambda j, i: (i // tps, 0, j)),
                   pl.BlockSpec((ms, tf), lambda j, i: (0, j)),
                   pl.BlockSpec((ms, tf), lambda j, i: (0, j))),
        scratch_shapes=[pltpu.VMEM((k, tf), BF16),
                        pltpu.VMEM((k, tf), BF16),
                        pltpu.VMEM((tm + SUBLANES, tf), F32)],
        compiler_params=_params("arbitrary", "arbitrary"),
        name="ffn_in",
    )(x, w_in, w_in, xs, conv_w, conv_b[:, None, :])


def _ffn_gate(a, b, buf, conv_w, conv_b, layer, batch):
    m, f = a.shape
    t = m // batch
    row_spec = pl.BlockSpec((None, t, f), lambda i: (i, 0, 0))
    hist_spec = pl.BlockSpec((None, FFN_CONV_W - 1, f), lambda i: (i, 0, 0))
    out, st = pl.pallas_call(
        _ffn_gate_kernel,
        out_shape=(jax.ShapeDtypeStruct((batch, t, f), F32),
                   jax.ShapeDtypeStruct((batch, FFN_CONV_W - 1, f), F32)),
        grid=(batch,),
        in_specs=[row_spec,
                  row_spec,
                  pl.BlockSpec((None, None, FFN_CONV_W - 1, f), lambda i: (layer, i, 0, 0)),
                  pl.BlockSpec((None, FFN_CONV_W, f), lambda i: (layer, 0, 0)),
                  pl.BlockSpec((None, 1, f), lambda i: (layer, 0, 0))],
        out_specs=(row_spec, hist_spec),
        scratch_shapes=[pltpu.VMEM((t + SUBLANES, f), F32)],
        compiler_params=_params("arbitrary"),
        name="ffn_gate",
    )(a.reshape(batch, t, f), b.reshape(batch, t, f), buf, conv_w, conv_b[:, None, :])
    return out.reshape(m, f), st


def kernel(x_prompt, x_sample, cache_sb_k, cache_sb_v, page_table, state_hgrn, state_ffn_conv,
           norm_mixer, norm_ffn, w_sb_qkv, sb_q_gain, sb_k_gain, sb_logit_bias, w_sb_o,
           w_hg_in, hg_lower_bounds, hg_out_gain, w_hg_o,
           w_ffn_in, ffn_conv_w, ffn_conv_b, w_ffn_out):
    bp, tp, d = x_prompt.shape
    bs, ts, _ = x_sample.shape
    depth = norm_mixer.shape[0]
    heads = d // HEAD_DIM
    n_mixers = 2
    n_sb, pool, page = cache_sb_k.shape[:3]
    cache_k = cache_sb_k.reshape(n_sb, pool, page * heads, HEAD_DIM)
    cache_v = cache_sb_v.reshape(n_sb, pool, page * heads, HEAD_DIM)
    lbp = _lower_bounds(hg_lower_bounds)

    xp = x_prompt.reshape(bp * tp, d)
    xs = x_sample.reshape(bs * ts, d)
    kv_p = kv_s = None
    sp_l, ss_l, cp_l, cs_l = [], [], [], []
    for layer in range(depth):
        j = layer // n_mixers
        hp = _rmsnorm(xp, norm_mixer, layer)
        hs = _rmsnorm(xs, norm_mixer, layer)
        if layer % n_mixers == 0:
            qkv_p, qkv_s = _matmul(hp, hs, w_sb_qkv, j, sections=3)
            kb_p, vb_p, *kv_p = _kv_heads(qkv_p, sb_k_gain, j, n_sb, kv_p)
            _, _, *kv_s = _kv_heads(qkv_s, sb_k_gain, j, n_sb, kv_s)
            op = _sb_prompt(qkv_p, kb_p, vb_p, sb_q_gain, sb_logit_bias, j, bp)
            os_ = _sb_sample(qkv_s, kv_s[0], kv_s[1], sb_q_gain, sb_logit_bias, cache_k, cache_v,
                             page_table, j, bs)
            xp, xs = _matmul(op, os_, w_sb_o, j, res=xp, ress=xs)
        else:
            proj_p, proj_s = _matmul(hp, hs, w_hg_in, j)
            op, sp = _hg_prompt(proj_p[0], lbp, hg_out_gain, layer, j, bp)
            os_, ss = _hg_sample(proj_s[0], lbp, hg_out_gain, state_hgrn, layer, j, bs)
            xp, xs = _matmul(op, os_, w_hg_o, j, res=xp, ress=xs)
            sp_l.append(sp)
            ss_l.append(ss)
        xp, xs = xp[0], xs[0]
        hp = _rmsnorm(xp, norm_ffn, layer)
        hs = _rmsnorm(xs, norm_ffn, layer)
        gp, cp, a_s, b_s = _ffn_in(hp, hs, w_ffn_in, ffn_conv_w, ffn_conv_b, layer, bp)
        gs, cs = _ffn_gate(a_s, b_s, state_ffn_conv, ffn_conv_w, ffn_conv_b, layer, bs)
        xp, xs = _matmul(gp, gs, w_ffn_out, layer, res=xp, ress=xs, tm_target=512, tn_target=512)
        xp, xs = xp[0], xs[0]
        cp_l.append(cp)
        cs_l.append(cs)

    kv_shape_p = (n_sb, bp, tp, heads, HEAD_DIM)
    kv_shape_s = (n_sb, bs, ts, heads, HEAD_DIM)
    return (xp.reshape(bp, tp, d), xs.reshape(bs, ts, d),
            kv_p[0].reshape(kv_shape_p), kv_p[1].reshape(kv_shape_p),
            kv_s[0].reshape(kv_shape_s), kv_s[1].reshape(kv_shape_s),
            jnp.stack(sp_l), jnp.stack(ss_l), jnp.stack(cp_l), jnp.stack(cs_l))
```

```python
import functools

import jax
import jax.numpy as jnp
import numpy as np
from jax import lax
from jax.experimental import pallas as pl
from jax.experimental.pallas import tpu as pltpu

F32 = jnp.float32
BF16 = jnp.bfloat16

HEAD_DIM = 128
SUBLANES = 8
NORM_EPS = 1e-6
FFN_CONV_W = 3
SB_TILE = 256
SB_HEADS_PER_STEP = 4
SB_PAGES_PER_STEP = 4
FFN_ROW_BLOCKS = 8
HG_HEADS_PER_STEP = 4
HG_CHUNK = 64
HG_SUB = 16
VMEM_LIMIT_BYTES = 56 * 1024 * 1024


def _params(*semantics):
    return pltpu.CompilerParams(dimension_semantics=semantics, vmem_limit_bytes=VMEM_LIMIT_BYTES)


def _pick(n, target, quantum=HEAD_DIM):
    best = None
    for d in range(quantum, min(n, target) + 1, quantum):
        if n % d == 0:
            best = d
    return best if best is not None else n


LOG2_E = 1.4426950408889634


def _softplus_log2(z):
    sign_bit = jnp.int32(-2 ** 31)
    neg_abs = lax.bitcast_convert_type(lax.bitcast_convert_type(z, jnp.int32) | sign_bit, F32)
    return jnp.maximum(z, 0.0) + jnp.log2(1.0 + jnp.exp2(neg_abs))


def _split_bf16(x, parts):
    out = []
    r = x
    for _ in range(parts - 1):
        p = r.astype(BF16)
        out.append(p)
        r = r - p.astype(F32)
    out.append(r.astype(BF16))
    return out


def _rmsnorm_kernel(x_ref, g_ref, o_ref):
    x = x_ref[...]
    ms = jnp.mean(x * x, axis=-1, keepdims=True)
    o_ref[...] = (x * lax.rsqrt(ms + NORM_EPS) * g_ref[...]).astype(o_ref.dtype)


def _rmsnorm(x, gains, layer):
    m, d = x.shape
    tm = _pick(m, 512, SUBLANES)
    return pl.pallas_call(
        _rmsnorm_kernel,
        out_shape=jax.ShapeDtypeStruct((m, d), BF16),
        grid=(m // tm,),
        in_specs=[pl.BlockSpec((tm, d), lambda i: (i, 0)),
                  pl.BlockSpec((None, 1, d), lambda i: (layer, 0, 0))],
        out_specs=pl.BlockSpec((tm, d), lambda i: (i, 0)),
        compiler_params=_params("arbitrary"),
        name="rmsnorm",
    )(x, gains[:, None, :])


def _matmul_kernel(*refs, has_res):
    if has_res:
        x_ref, w_ref, r_ref, xs_ref, rs_ref, o_ref, os_ref, wb_ref = refs
    else:
        x_ref, w_ref, xs_ref, o_ref, os_ref, wb_ref = refs

    @pl.when(pl.program_id(1) == 0)
    def _():
        wb_ref[...] = w_ref[...].astype(BF16)
        acc_s = jnp.dot(xs_ref[...].astype(BF16), wb_ref[...], preferred_element_type=F32)
        if has_res:
            acc_s = acc_s + rs_ref[...]
        os_ref[...] = acc_s

    acc = jnp.dot(x_ref[...].astype(BF16), wb_ref[...], preferred_element_type=F32)
    if has_res:
        acc = acc + r_ref[...]
    o_ref[...] = acc.astype(o_ref.dtype)


def _matmul(x, xs, w, layer, *, res=None, ress=None, sections=1, tm_target=1024, tn_target=1024):
    m, k = x.shape
    ms = xs.shape[0]
    n = w.shape[-1]
    ns = n // sections
    tm = _pick(m, tm_target, SUBLANES)
    tn = _pick(ns, tn_target)
    per = ns // tn
    in_specs = [pl.BlockSpec((tm, k), lambda j, i: (i, 0)),
                pl.BlockSpec((None, k, tn), lambda j, i: (layer, 0, j))]
    args = [x, w]
    if res is not None:
        assert sections == 1
        in_specs.append(pl.BlockSpec((tm, tn), lambda j, i: (i, j)))
        args.append(res)
    in_specs.append(pl.BlockSpec((ms, k), lambda j, i: (0, 0)))
    args.append(xs)
    if res is not None:
        in_specs.append(pl.BlockSpec((ms, tn), lambda j, i: (0, j)))
        args.append(ress)
    return pl.pallas_call(
        functools.partial(_matmul_kernel, has_res=res is not None),
        out_shape=(jax.ShapeDtypeStruct((sections, m, ns), F32),
                   jax.ShapeDtypeStruct((sections, ms, ns), F32)),
        grid=(n // tn, m // tm),
        in_specs=in_specs,
        out_specs=(pl.BlockSpec((None, tm, tn), lambda j, i: (j // per, i, j % per)),
                   pl.BlockSpec((None, ms, tn), lambda j, i: (j // per, 0, j % per))),
        scratch_shapes=[pltpu.VMEM((k, tn), BF16)],
        compiler_params=_params("arbitrary", "arbitrary"),
        name="matmul",
    )(*args)


def _kv_heads_kernel(k_ref, v_ref, g_ref, *rest, heads):
    kb_ref, vb_ref, knh_ref, vh_ref = rest[-4:]
    g = g_ref[...]
    tm = k_ref.shape[0]
    for h in range(heads):
        cols = slice(h * HEAD_DIM, (h + 1) * HEAD_DIM)
        x = k_ref[:, cols]
        ms = jnp.mean(x * x, axis=-1, keepdims=True)
        kn = x * lax.rsqrt(ms + NORM_EPS) * g
        v = v_ref[:, cols]
        kb_ref[:, cols] = kn.astype(BF16)
        vb_ref[:, cols] = v.astype(BF16)
        knh_ref[pl.ds(h, tm, stride=heads), :] = kn
        vh_ref[pl.ds(h, tm, stride=heads), :] = v


def _kv_heads(qkv3, gains, layer, n_layers, stacks):
    _, m, d = qkv3.shape
    heads = d // HEAD_DIM
    tm = _pick(m, 256, SUBLANES)
    flat = jax.ShapeDtypeStruct((m, d), BF16)
    by_head = jax.ShapeDtypeStruct((n_layers, m * heads, HEAD_DIM), F32)
    flat_spec = pl.BlockSpec((tm, d), lambda i: (i, 0))
    head_spec = pl.BlockSpec((None, tm * heads, HEAD_DIM), lambda i: (layer, i, 0))
    in_specs = [pl.BlockSpec((None, tm, d), lambda i: (1, i, 0)),
                pl.BlockSpec((None, tm, d), lambda i: (2, i, 0)),
                pl.BlockSpec((None, 1, HEAD_DIM), lambda i: (layer, 0, 0))]
    args = [qkv3, qkv3, gains[:, None, :]]
    aliases = {}
    if stacks is not None:
        in_specs += [pl.BlockSpec(memory_space=pl.ANY)] * 2
        args += list(stacks)
        aliases = {3: 2, 4: 3}
    return pl.pallas_call(
        functools.partial(_kv_heads_kernel, heads=heads),
        out_shape=(flat, flat, by_head, by_head),
        grid=(m // tm,),
        in_specs=in_specs,
        out_specs=(flat_spec, flat_spec, head_spec, head_spec),
        input_output_aliases=aliases,
        compiler_params=_params("arbitrary"),
        name="kv_heads",
    )(*args)


def _suffix_ones(n):
    r = lax.broadcasted_iota(jnp.int32, (n, n), 0)
    c = lax.broadcasted_iota(jnp.int32, (n, n), 1)
    return jnp.where(r > c, 1.0, 0.0).astype(BF16)


def _nt_dot(a, b):
    return lax.dot_general(a, b, (((1,), (1,)), ((), ())), preferred_element_type=F32)


def _sb_log_keep(z, mask):
    sp = _softplus_log2(z)
    lk = -sp if mask is None else jnp.where(mask, -sp, 0.0)
    return lk, z - sp


def _sb_tail(lk, carry, u):
    tail = carry
    for part in _split_bf16(lk, 2):
        tail = tail + jnp.dot(part, u, preferred_element_type=F32)
    return tail


def _normed_query(q, gain):
    ms = jnp.mean(q * q, axis=-1, keepdims=True)
    return q * lax.rsqrt(ms + NORM_EPS) * gain * (HEAD_DIM ** -0.5 * LOG2_E)


def _sb_prompt_kernel(bias_ref, q_ref, k_ref, v_ref, qg_ref, o_ref, qb_ref, acc_ref, carry_ref,
                      *, group):
    hg = pl.program_id(1)
    i = pl.program_id(2)
    tile = q_ref.shape[0]
    u = _suffix_ones(tile)
    row = lax.broadcasted_iota(jnp.int32, (tile, tile), 0)
    col = lax.broadcasted_iota(jnp.int32, (tile, tile), 1)
    for g in range(group):
        cols = slice(g * HEAD_DIM, (g + 1) * HEAD_DIM)
        qb_ref[:, cols] = _normed_query(q_ref[:, cols], qg_ref[...]).astype(BF16)
    acc_ref[...] = jnp.zeros_like(acc_ref)
    carry_ref[...] = jnp.zeros_like(carry_ref)

    def key_tile(j, mask):
        keys = pl.ds(pl.multiple_of(j * tile, tile), tile)
        heads = [slice(g * HEAD_DIM, (g + 1) * HEAD_DIM) for g in range(group)]
        zs = [_nt_dot(qb_ref[:, cols], k_ref[keys, cols]) + bias_ref[hg * group + g] * LOG2_E
              for g, cols in enumerate(heads)]
        keeps = [_sb_log_keep(z, mask) for z in zs]
        tails = [_sb_tail(lk, carry_ref[g], u) for g, (lk, _) in enumerate(keeps)]
        for g, cols in enumerate(heads):
            lk, log_beta = keeps[g]
            a = jnp.exp2(log_beta + tails[g])
            if mask is not None:
                a = jnp.where(mask, a, 0.0)
            acc_ref[:, cols] += jnp.dot(a.astype(BF16), v_ref[keys, cols],
                                        preferred_element_type=F32)
            carry_ref[g] += jnp.sum(lk, axis=-1, keepdims=True)

    key_tile(i, col < row)

    def body(step, _):
        key_tile(i - 1 - step, None)
        return 0

    lax.fori_loop(0, i, body, 0)
    o_ref[...] = acc_ref[...].astype(o_ref.dtype)


def _sb_prompt(qkv3, kb, vb, q_gain, bias, layer, batch):
    _, m, d = qkv3.shape
    t = m // batch
    heads = d // HEAD_DIM
    tile = _pick(t, SB_TILE)
    nq = t // tile
    group = SB_HEADS_PER_STEP if heads % SB_HEADS_PER_STEP == 0 else 1
    width = group * HEAD_DIM
    kv_spec = pl.BlockSpec((t, width), lambda b, h, i: (b, h))
    return pl.pallas_call(
        functools.partial(_sb_prompt_kernel, group=group),
        out_shape=jax.ShapeDtypeStruct((m, d), BF16),
        grid=(batch, heads // group, nq),
        in_specs=[pl.BlockSpec(memory_space=pltpu.SMEM),
                  pl.BlockSpec((None, tile, width), lambda b, h, i: (0, b * nq + i, h)),
                  kv_spec,
                  kv_spec,
                  pl.BlockSpec((None, 1, HEAD_DIM), lambda b, h, i: (layer, 0, 0))],
        out_specs=pl.BlockSpec((tile, width), lambda b, h, i: (b * nq + i, h)),
        scratch_shapes=[pltpu.VMEM((tile, width), BF16),
                        pltpu.VMEM((tile, width), F32),
                        pltpu.VMEM((group, tile, 1), F32)],
        compiler_params=_params("arbitrary", "arbitrary", "arbitrary"),
        name="sb_prompt",
    )(bias[layer], qkv3, kb, vb, q_gain[:, None, :])


def _sb_sample_kernel(pt_ref, q_ref, kn_ref, vn_ref, qg_ref, bias_ref, hmask_ref, wsuf_ref, *rest,
                      heads, tq, page, pages_per_step):
    del pt_ref
    kc_refs = rest[:pages_per_step]
    vc_refs = rest[pages_per_step:2 * pages_per_step]
    o_ref, qall_ref, acc_ref, carry_ref, kpad_ref, vpad_ref = rest[2 * pages_per_step:]
    p = pl.program_id(1)
    lanes = page * heads
    n_blk = lanes // HEAD_DIM
    n_grp = heads * tq // SUBLANES

    def tiles(kv_refs, valid):
        logits = []
        for k_ref, _ in kv_refs:
            f = _nt_dot(qall_ref[...].astype(BF16), k_ref[...].astype(BF16)) * hmask_ref[...]
            zs = f[0:SUBLANES]
            for g in range(1, n_grp):
                zs = zs + f[g * SUBLANES:(g + 1) * SUBLANES]
            z = zs
            for k in range(1, SUBLANES // tq):
                z = z + pltpu.roll(zs, k * tq, axis=0)
            z = z + jnp.concatenate([bias_ref[...] * LOG2_E] * n_blk, axis=1)
            logits.append(_sb_log_keep(z, valid))
        sums = []
        for lk, _ in logits:
            blocks = jnp.concatenate(
                [lk[:, b * HEAD_DIM:(b + 1) * HEAD_DIM] for b in range(n_blk)], axis=0)
            res = jnp.zeros((n_blk * SUBLANES, 2 * HEAD_DIM), F32)
            for part in _split_bf16(blocks, 2):
                res = res + jnp.dot(part, wsuf_ref[...], preferred_element_type=F32)
            sums.append(res)
        run = carry_ref[...]
        out = jnp.zeros(acc_ref.shape, F32)
        for (_, log_beta), res, (_, v_ref) in zip(logits, sums, kv_refs):
            tails = [None] * n_blk
            for b in reversed(range(n_blk)):
                blk = res[b * SUBLANES:(b + 1) * SUBLANES]
                tails[b] = blk[:, :HEAD_DIM] + run
                run = run + blk[:, HEAD_DIM:]
            a = jnp.exp2(log_beta + jnp.concatenate(tails, axis=1))
            if valid is not None:
                a = jnp.where(valid, a, 0.0)
            a_heads = (jnp.concatenate([a] * n_grp, axis=0) * hmask_ref[...]).astype(BF16)
            out = out + jnp.dot(a_heads, v_ref[...].astype(BF16), preferred_element_type=F32)
        carry_ref[...] = run
        acc_ref[...] += out

    @pl.when(p == 0)
    def _new_tokens():
        for h in range(heads):
            cols = slice(h * HEAD_DIM, (h + 1) * HEAD_DIM)
            qall_ref[h * tq:(h + 1) * tq, :] = _normed_query(q_ref[:, cols], qg_ref[...])
        kpad_ref[...] = jnp.zeros_like(kpad_ref)
        vpad_ref[...] = jnp.zeros_like(vpad_ref)
        kpad_ref[0:tq * heads, :] = kn_ref[...]
        vpad_ref[0:tq * heads, :] = vn_ref[...]
        acc_ref[...] = jnp.zeros_like(acc_ref)
        carry_ref[...] = jnp.zeros_like(carry_ref)
        query = lax.broadcasted_iota(jnp.int32, (SUBLANES, lanes), 0) % tq
        position = lax.broadcasted_iota(jnp.int32, (SUBLANES, lanes), 1) // heads
        tiles([(kpad_ref, vpad_ref)], position < query)

    @pl.when(p > 0)
    def _past_page():
        tiles(list(zip(kc_refs, vc_refs)), None)

    @pl.when(p == pl.num_programs(1) - 1)
    def _finish():
        for h in range(heads):
            o_ref[:, h * HEAD_DIM:(h + 1) * HEAD_DIM] = acc_ref[h * tq:(h + 1) * tq, :]


def _sb_sample(qkv3, kn, vh, q_gain, bias, cache_k, cache_v, page_table, layer, batch):
    _, m, d = qkv3.shape
    tq = m // batch
    heads = d // HEAD_DIM
    assert SUBLANES % tq == 0 and (heads * tq) % SUBLANES == 0 and HEAD_DIM % heads == 0
    rows = heads * tq
    n_pages = page_table.shape[1]
    lanes = cache_k.shape[2]
    page = lanes // heads
    qkv4 = qkv3.reshape(3, batch, tq, d)
    lane_head = np.arange(lanes) % heads
    bias_tile = jnp.broadcast_to(jnp.tile(bias[layer], HEAD_DIM // heads)[None, :],
                                 (SUBLANES, HEAD_DIM)).astype(F32)
    hmask = jnp.asarray(lane_head[None, :] == (np.arange(rows) // tq)[:, None], F32)
    l = np.arange(HEAD_DIM)
    same_head = (l[:, None] % heads) == (l[None, :] % heads)
    later = (l[:, None] // heads) > (l[None, :] // heads)
    wsuf = jnp.asarray(np.concatenate([same_head & later, same_head], axis=1), BF16)

    pps = SB_PAGES_PER_STEP if n_pages % SB_PAGES_PER_STEP == 0 else 1

    def page_index(j):
        return lambda b, p, pt: (layer, pt[b, n_pages - 1 - ((jnp.maximum(p, 1) - 1) * pps + j)], 0, 0)

    page_specs = [pl.BlockSpec((None, None, lanes, HEAD_DIM), page_index(j)) for j in range(pps)]
    fixed = lambda shape: pl.BlockSpec(shape, lambda b, p, pt: (0,) * len(shape))
    new_spec = pl.BlockSpec((None, None, tq * heads, HEAD_DIM), lambda b, p, pt: (layer, b, 0, 0))
    stack_shape = (kn.shape[0], batch, tq * heads, HEAD_DIM)
    grid_spec = pltpu.PrefetchScalarGridSpec(
        num_scalar_prefetch=1,
        grid=(batch, n_pages // pps + 1),
        in_specs=[pl.BlockSpec((None, None, tq, d), lambda b, p, pt: (0, b, 0, 0)),
                  new_spec,
                  new_spec,
                  pl.BlockSpec((None, 1, HEAD_DIM), lambda b, p, pt: (layer, 0, 0)),
                  fixed(bias_tile.shape),
                  fixed(hmask.shape),
                  fixed(wsuf.shape)] + page_specs + page_specs,
        out_specs=pl.BlockSpec((None, tq, d), lambda b, p, pt: (b, 0, 0)),
        scratch_shapes=[pltpu.VMEM((rows, HEAD_DIM), F32),
                        pltpu.VMEM((rows, HEAD_DIM), F32),
                        pltpu.VMEM((SUBLANES, HEAD_DIM), F32),
                        pltpu.VMEM((lanes, HEAD_DIM), F32),
                        pltpu.VMEM((lanes, HEAD_DIM), F32)])
    out = pl.pallas_call(
        functools.partial(_sb_sample_kernel, heads=heads, tq=tq, page=page, pages_per_step=pps),
        out_shape=jax.ShapeDtypeStruct((batch, tq, d), F32),
        grid_spec=grid_spec,
        compiler_params=_params("arbitrary", "arbitrary"),
        name="sb_sample",
    )(page_table, qkv4, kn.reshape(stack_shape), vh.reshape(stack_shape),
      q_gain[:, None, :], bias_tile, hmask, wsuf, *([cache_k] * pps), *([cache_v] * pps))
    return out.reshape(m, d)


def _lower_bounds_kernel(x_ref, lb_ref):
    x = x_ref[...]
    e = jnp.exp(x - jnp.max(x, axis=0, keepdims=True))
    soft = e / jnp.sum(e, axis=0, keepdims=True)
    depth = x.shape[0]
    run = jnp.zeros_like(soft[0:1])
    for layer in range(depth):
        run = run + soft[layer:layer + 1]
        lb = run - soft[0:1]
        lb_ref[layer, 0:1, :] = lb
        lb_ref[layer, 1:2, :] = jnp.log(lb)
        lb_ref[layer, 2:3, :] = jnp.log1p(-lb)
        lb_ref[layer, 3:4, :] = 1.0 - lb
        lb_ref[layer, 4:8, :] = jnp.zeros((4, x.shape[1]), F32)


def _lower_bounds(hg_lower_bounds):
    depth, d = hg_lower_bounds.shape
    return pl.pallas_call(
        _lower_bounds_kernel,
        out_shape=jax.ShapeDtypeStruct((depth, SUBLANES, d), F32),
        name="hg_lower_bounds",
    )(hg_lower_bounds.astype(F32))


def _hg_gates(f, lbp):
    log_lb, log_1m, one_m = lbp[1:2], lbp[2:3], lbp[3:4]
    log_sig = jnp.minimum(f, 0.0) - jnp.log(1.0 + jnp.exp(-jnp.abs(f)))
    b = log_1m + log_sig
    g = jnp.maximum(log_lb, b) + jnp.log(1.0 + jnp.exp(-jnp.abs(log_lb - b)))
    key = one_m * (1.0 / (1.0 + jnp.exp(f)))
    return g, key


def _hg_output(o, gate, gain):
    ms = jnp.mean(o * o, axis=-1, keepdims=True)
    on = o * lax.rsqrt(ms + NORM_EPS) * gain
    return on * (gate * (1.0 / (1.0 + jnp.exp(-gate))))


def _hg_prompt_kernel(q_ref, f_ref, i_ref, g_ref, lbp_ref, gain_ref, o_ref, s_ref,
                      st_ref, gc_ref, *, chunk, sub, group):
    t = q_ref.shape[0]
    n_sub = chunk // sub
    gain = gain_ref[...]
    r = lax.broadcasted_iota(jnp.int32, (chunk, chunk), 0)
    c = lax.broadcasted_iota(jnp.int32, (chunk, chunk), 1)
    lower = jnp.where(c <= r, 1.0, 0.0).astype(BF16)
    crow = lax.broadcasted_iota(jnp.int32, (chunk, HEAD_DIM), 0)
    srow = lax.broadcasted_iota(jnp.int32, (sub, HEAD_DIM), 0)
    own = (lax.broadcasted_iota(jnp.int32, (sub, sub * sub), 1) // sub
           == lax.broadcasted_iota(jnp.int32, (sub, sub * sub), 0))
    st_ref[...] = jnp.zeros_like(st_ref)

    def head_cols(hi):
        return slice(hi * HEAD_DIM, (hi + 1) * HEAD_DIM)

    def decay(r0, hi):
        glog, kk = _hg_gates(f_ref[pl.ds(r0, chunk), head_cols(hi)], lbp_ref[:, head_cols(hi)])
        gc = jnp.zeros((chunk, HEAD_DIM), F32)
        for part in _split_bf16(glog, 3):
            gc = gc + jnp.dot(lower, part, preferred_element_type=F32)
        return gc, kk

    def scores(r0, hi, gc, kk):
        rows = pl.ds(r0, chunk)
        gc_ref[hi] = gc
        q = q_ref[rows, head_cols(hi)]
        vb = i_ref[rows, head_cols(hi)].astype(BF16)
        st = st_ref[hi]
        inter = _nt_dot((q * jnp.exp(gc)).astype(BF16), st.astype(BF16))
        g_last = gc[chunk - 1:chunk, :]
        offs, diags = [], []
        for si in range(n_sub):
            base = si * sub
            blk = slice(base, base + sub)
            q_s, g_s, k_s = q[blk], gc[blk], kk[blk]
            if si > 0:
                g_start = gc[base - 1:base, :]
                k_prev = jnp.where(crow < base, kk * jnp.exp(jnp.minimum(g_start - gc, 0.0)), 0.0)
                q_in = q_s * jnp.exp(g_s - g_start)
                offs.append(_nt_dot(q_in.astype(BF16), k_prev.astype(BF16)))
            else:
                offs.append(None)
            stacked = []
            for ti in range(sub):
                g_t = gc_ref[hi, base + ti:base + ti + 1, :]
                k_t = jnp.where(srow <= ti, k_s * jnp.exp(jnp.minimum(g_t - g_s, 0.0)), 0.0)
                stacked.append(k_t.astype(BF16))
            diags.append(_nt_dot(q_s.astype(BF16), jnp.concatenate(stacked, axis=0)))
        k_dec = kk * jnp.exp(g_last - gc)
        st_ref[hi] = st * jnp.exp(g_last) + lax.dot_general(
            vb, k_dec.astype(BF16), (((0,), (0,)), ((), ())), preferred_element_type=F32)
        return vb, inter, offs, diags

    def outputs(r0, hi, vb, inter, offs, diags):
        for si in range(n_sub):
            base = si * sub
            blk = slice(base, base + sub)
            o_s = inter[blk]
            if offs[si] is not None:
                o_s = o_s + jnp.dot(offs[si].astype(BF16), vb, preferred_element_type=F32)
            sc = jnp.where(own, diags[si], 0.0).astype(BF16)
            o_s = o_s + jnp.dot(sc, jnp.concatenate([vb[blk]] * sub, axis=0),
                                preferred_element_type=F32)
            out_rows = pl.ds(pl.multiple_of(r0 + base, sub), sub)
            o_ref[out_rows, head_cols(hi)] = _hg_output(
                o_s, g_ref[out_rows, head_cols(hi)], gain).astype(o_ref.dtype)

    def body(ci, _):
        r0 = pl.multiple_of(ci * chunk, chunk)
        decays = [decay(r0, hi) for hi in range(group)]
        staged = [scores(r0, hi, *decays[hi]) for hi in range(group)]
        for hi in range(group):
            outputs(r0, hi, *staged[hi])
        return 0

    lax.fori_loop(0, t // chunk, body, 0)
    for hi in range(group):
        s_ref[hi] = st_ref[hi].T


def _hg_prompt(proj, lbp, out_gain, layer, hg_index, batch):
    m, d4 = proj.shape
    d = d4 // 4
    heads = d // HEAD_DIM
    t = m // batch
    chunk = _pick(t, HG_CHUNK, HG_SUB)
    sub = HG_SUB
    group = HG_HEADS_PER_STEP if heads % HG_HEADS_PER_STEP == 0 else 1
    width = group * HEAD_DIM
    ng = heads // group
    col = lambda sec: (lambda b, h: (b, sec * ng + h))
    return pl.pallas_call(
        functools.partial(_hg_prompt_kernel, chunk=chunk, sub=sub, group=group),
        out_shape=(jax.ShapeDtypeStruct((m, d), BF16),
                   jax.ShapeDtypeStruct((batch, heads, HEAD_DIM, HEAD_DIM), F32)),
        grid=(batch, ng),
        in_specs=[pl.BlockSpec((t, width), col(0)),
                  pl.BlockSpec((t, width), col(1)),
                  pl.BlockSpec((t, width), col(2)),
                  pl.BlockSpec((t, width), col(3)),
                  pl.BlockSpec((None, SUBLANES, width), lambda b, h: (layer, 0, h)),
                  pl.BlockSpec((None, 1, HEAD_DIM), lambda b, h: (hg_index, 0, 0))],
        out_specs=(pl.BlockSpec((t, width), lambda b, h: (b, h)),
                   pl.BlockSpec((None, group, HEAD_DIM, HEAD_DIM), lambda b, h: (b, h, 0, 0))),
        scratch_shapes=[pltpu.VMEM((group, HEAD_DIM, HEAD_DIM), F32),
                        pltpu.VMEM((group, chunk, HEAD_DIM), F32)],
        compiler_params=_params("arbitrary", "arbitrary"),
        name="hg_prompt",
    )(proj, proj, proj, proj, lbp, out_gain[:, None, :])


def _hg_sample_kernel(q_ref, f_ref, i_ref, g_ref, lbp_ref, gain_ref, s0_ref, o_ref, s_ref, pad_ref,
                      *, heads):
    tq = q_ref.shape[0]
    pad_ref[...] = jnp.zeros_like(pad_ref)

    def columns(slot, x):
        pad_ref[slot, 0:tq, :] = x
        return pad_ref[slot].T

    for h in range(heads):
        cols = slice(h * HEAD_DIM, (h + 1) * HEAD_DIM)
        glog, kk = _hg_gates(f_ref[:, cols], lbp_ref[:, cols])
        q_c = columns(3 * h, q_ref[:, cols])
        f_c = columns(3 * h + 1, jnp.exp(glog))
        k_c = columns(3 * h + 2, kk)
        v = i_ref[:, cols]
        s = s0_ref[h]
        for ti in range(tq):
            s = f_c[:, ti:ti + 1] * s + k_c[:, ti:ti + 1] * v[ti:ti + 1, :]
            o = jnp.sum(q_c[:, ti:ti + 1] * s, axis=0, keepdims=True)
            o_ref[ti:ti + 1, cols] = _hg_output(o, g_ref[ti:ti + 1, cols], gain_ref[...])
        s_ref[h] = s


def _hg_sample(proj, lbp, out_gain, state, layer, hg_index, batch):
    m, d4 = proj.shape
    d = d4 // 4
    heads = d // HEAD_DIM
    tq = m // batch
    proj3 = proj.reshape(batch, tq, d4)
    col = lambda sec: (lambda b: (b, 0, sec))
    state_block = (None, heads, HEAD_DIM, HEAD_DIM)
    out, s_new = pl.pallas_call(
        functools.partial(_hg_sample_kernel, heads=heads),
        out_shape=(jax.ShapeDtypeStruct((batch, tq, d), F32),
                   jax.ShapeDtypeStruct((batch, heads, HEAD_DIM, HEAD_DIM), F32)),
        grid=(batch,),
        in_specs=[pl.BlockSpec((None, tq, d), col(0)),
                  pl.BlockSpec((None, tq, d), col(1)),
                  pl.BlockSpec((None, tq, d), col(2)),
                  pl.BlockSpec((None, tq, d), col(3)),
                  pl.BlockSpec((None, SUBLANES, d), lambda b: (layer, 0, 0)),
                  pl.BlockSpec((None, 1, HEAD_DIM), lambda b: (hg_index, 0, 0)),
                  pl.BlockSpec((None,) + state_block, lambda b: (hg_index, b, 0, 0, 0))],
        out_specs=(pl.BlockSpec((None, tq, d), lambda b: (b, 0, 0)),
                   pl.BlockSpec(state_block, lambda b: (b, 0, 0, 0))),
        scratch_shapes=[pltpu.VMEM((3 * heads, HEAD_DIM, HEAD_DIM), F32)],
        compiler_params=_params("arbitrary"),
        name="hg_sample",
    )(proj3, proj3, proj3, proj3, lbp, out_gain[:, None, :], state)
    return out.reshape(m, d), s_new


def _ffn_gate_kernel(a_ref, b_ref, buf_ref, w_ref, cb_ref, o_ref, st_ref, ext_ref):
    t = a_ref.shape[0]
    ext_ref[CONV_LEAD:SUBLANES, :] = buf_ref[...]
    o_ref[...] = _conv_gate(ext_ref, a_ref[...], b_ref[...], w_ref, cb_ref).astype(o_ref.dtype)
    st_ref[...] = ext_ref[SUBLANES + t - (FFN_CONV_W - 1):SUBLANES + t, :]


CONV_LEAD = SUBLANES - (FFN_CONV_W - 1)


def _conv_gate(ext_ref, a, b, w_ref, cb_ref):
    t = a.shape[0]
    lead = CONV_LEAD
    ext_ref[SUBLANES:SUBLANES + t, :] = a
    c = cb_ref[...]
    for j in range(FFN_CONV_W - 1):
        c = c + w_ref[j:j + 1, :] * ext_ref[lead + j:lead + j + t, :]
    c = c + w_ref[FFN_CONV_W - 1:FFN_CONV_W, :] * a
    return c * (1.0 / (1.0 + jnp.exp(-c))) * b


def _ffn_in_kernel(x_ref, wa_ref, wb_ref, xs_ref, cw_ref, cb_ref, o_ref, st_ref, as_ref, bs_ref,
                   wa16_ref, wb16_ref, ext_ref, *, tiles_per_seq, row_blocks):
    i = pl.program_id(1)
    tm = x_ref.shape[0]

    @pl.when(i == 0)
    def _():
        wa16_ref[...] = wa_ref[...].astype(BF16)
        wb16_ref[...] = wb_ref[...].astype(BF16)
        xs = xs_ref[...].astype(BF16)
        as_ref[...] = jnp.dot(xs, wa16_ref[...], preferred_element_type=F32)
        bs_ref[...] = jnp.dot(xs, wb16_ref[...], preferred_element_type=F32)

    @pl.when(i % tiles_per_seq == 0)
    def _():
        ext_ref[CONV_LEAD:SUBLANES, :] = jnp.zeros((FFN_CONV_W - 1, ext_ref.shape[1]), F32)

    @pl.when(i % tiles_per_seq != 0)
    def _():
        ext_ref[CONV_LEAD:SUBLANES, :] = ext_ref[SUBLANES + tm - (FFN_CONV_W - 1):SUBLANES + tm, :]

    rb = tm // row_blocks

    def products(r):
        x = x_ref[r * rb:(r + 1) * rb, :]
        a = jnp.dot(x, wa16_ref[...], preferred_element_type=F32)
        ext_ref[SUBLANES + r * rb:SUBLANES + (r + 1) * rb, :] = a
        return a, jnp.dot(x, wb16_ref[...], preferred_element_type=F32)

    def gate(r, a, b):
        lo = CONV_LEAD + r * rb
        c = cb_ref[...]
        for j in range(FFN_CONV_W - 1):
            c = c + cw_ref[j:j + 1, :] * ext_ref[lo + j:lo + j + rb, :]
        c = c + cw_ref[FFN_CONV_W - 1:FFN_CONV_W, :] * a
        o_ref[r * rb:(r + 1) * rb, :] = (c * (1.0 / (1.0 + jnp.exp(-c))) * b).astype(o_ref.dtype)

    pending = products(0)
    for r in range(1, row_blocks):
        issued = products(r)
        gate(r - 1, *pending)
        pending = issued
    gate(row_blocks - 1, *pending)
    st_ref[...] = ext_ref[SUBLANES + tm - (FFN_CONV_W - 1):SUBLANES + tm, :]


def _ffn_in(x, xs, w_in, conv_w, conv_b, layer, batch):
    m, k = x.shape
    ms = xs.shape[0]
    f = w_in.shape[-1] // 2
    t = m // batch
    tm = _pick(t, 1024, SUBLANES)
    tps = t // tm
    tf = _pick(f, 512)
    nf = f // tf
    return pl.pallas_call(
        functools.partial(_ffn_in_kernel, tiles_per_seq=tps,
                          row_blocks=FFN_ROW_BLOCKS if tm % (FFN_ROW_BLOCKS * SUBLANES) == 0 else 1),
        out_shape=(jax.ShapeDtypeStruct((m, f), BF16),
                   jax.ShapeDtypeStruct((batch, FFN_CONV_W - 1, f), F32),
                   jax.ShapeDtypeStruct((ms, f), F32),
                   jax.ShapeDtypeStruct((ms, f), F32)),
        grid=(nf, m // tm),
        in_specs=[pl.BlockSpec((tm, k), lambda j, i: (i, 0)),
                  pl.BlockSpec((None, k, tf), lambda j, i: (layer, 0, j)),
                  pl.BlockSpec((None, k, tf), lambda j, i: (layer, 0, nf + j)),
                  pl.BlockSpec((ms, k), lambda j, i: (0, 0)),
                  pl.BlockSpec((None, FFN_CONV_W, tf), lambda j, i: (layer, 0, j)),
                  pl.BlockSpec((None, 1, tf), lambda j, i: (layer, 0, j))],
        out_specs=(pl.BlockSpec((tm, tf), lambda j, i: (i, j)),
                   pl.BlockSpec((None, FFN_CONV_W - 1, tf), lambda j, i: (i // tps, 0, j)),
                   pl.BlockSpec((ms, tf), lambda j, i: (0, j)),
                   pl.BlockSpec((ms, tf), lambda j, i: (0, j))),
        scratch_shapes=[pltpu.VMEM((k, tf), BF16),
                        pltpu.VMEM((k, tf), BF16),
                        pltpu.VMEM((tm + SUBLANES, tf), F32)],
        compiler_params=_params("arbitrary", "arbitrary"),
        name="ffn_in",
    )(x, w_in, w_in, xs, conv_w, conv_b[:, None, :])


def _ffn_gate(a, b, buf, conv_w, conv_b, layer, batch):
    m, f = a.shape
    t = m // batch
    row_spec = pl.BlockSpec((None, t, f), lambda i: (i, 0, 0))
    hist_spec = pl.BlockSpec((None, FFN_CONV_W - 1, f), lambda i: (i, 0, 0))
    out, st = pl.pallas_call(
        _ffn_gate_kernel,
        out_shape=(jax.ShapeDtypeStruct((batch, t, f), F32),
                   jax.ShapeDtypeStruct((batch, FFN_CONV_W - 1, f), F32)),
        grid=(batch,),
        in_specs=[row_spec,
                  row_spec,
                  pl.BlockSpec((None, None, FFN_CONV_W - 1, f), lambda i: (layer, i, 0, 0)),
                  pl.BlockSpec((None, FFN_CONV_W, f), lambda i: (layer, 0, 0)),
                  pl.BlockSpec((None, 1, f), lambda i: (layer, 0, 0))],
        out_specs=(row_spec, hist_spec),
        scratch_shapes=[pltpu.VMEM((t + SUBLANES, f), F32)],
        compiler_params=_params("arbitrary"),
        name="ffn_gate",
    )(a.reshape(batch, t, f), b.reshape(batch, t, f), buf, conv_w, conv_b[:, None, :])
    return out.reshape(m, f), st


def kernel(x_prompt, x_sample, cache_sb_k, cache_sb_v, page_table, state_hgrn, state_ffn_conv,
           norm_mixer, norm_ffn, w_sb_qkv, sb_q_gain, sb_k_gain, sb_logit_bias, w_sb_o,
           w_hg_in, hg_lower_bounds, hg_out_gain, w_hg_o,
           w_ffn_in, ffn_conv_w, ffn_conv_b, w_ffn_out):
    bp, tp, d = x_prompt.shape
    bs, ts, _ = x_sample.shape
    depth = norm_mixer.shape[0]
    heads = d // HEAD_DIM
    n_mixers = 2
    n_sb, pool, page = cache_sb_k.shape[:3]
    cache_k = cache_sb_k.reshape(n_sb, pool, page * heads, HEAD_DIM)
    cache_v = cache_sb_v.reshape(n_sb, pool, page * heads, HEAD_DIM)
    lbp = _lower_bounds(hg_lower_bounds)

    xp = x_prompt.reshape(bp * tp, d)
    xs = x_sample.reshape(bs * ts, d)
    kv_p = kv_s = None
    sp_l, ss_l, cp_l, cs_l = [], [], [], []
    for layer in range(depth):
        j = layer // n_mixers
        hp = _rmsnorm(xp, norm_mixer, layer)
        hs = _rmsnorm(xs, norm_mixer, layer)
        if layer % n_mixers == 0:
            qkv_p, qkv_s = _matmul(hp, hs, w_sb_qkv, j, sections=3)
            kb_p, vb_p, *kv_p = _kv_heads(qkv_p, sb_k_gain, j, n_sb, kv_p)
            _, _, *kv_s = _kv_heads(qkv_s, sb_k_gain, j, n_sb, kv_s)
            op = _sb_prompt(qkv_p, kb_p, vb_p, sb_q_gain, sb_logit_bias, j, bp)
            os_ = _sb_sample(qkv_s, kv_s[0], kv_s[1], sb_q_gain, sb_logit_bias, cache_k, cache_v,
                             page_table, j, bs)
            xp, xs = _matmul(op, os_, w_sb_o, j, res=xp, ress=xs)
        else:
            proj_p, proj_s = _matmul(hp, hs, w_hg_in, j)
            op, sp = _hg_prompt(proj_p[0], lbp, hg_out_gain, layer, j, bp)
            os_, ss = _hg_sample(proj_s[0], lbp, hg_out_gain, state_hgrn, layer, j, bs)
            xp, xs = _matmul(op, os_, w_hg_o, j, res=xp, ress=xs)
            sp_l.append(sp)
            ss_l.append(ss)
        xp, xs = xp[0], xs[0]
        hp = _rmsnorm(xp, norm_ffn, layer)
        hs = _rmsnorm(xs, norm_ffn, layer)
        gp, cp, a_s, b_s = _ffn_in(hp, hs, w_ffn_in, ffn_conv_w, ffn_conv_b, layer, bp)
        gs, cs = _ffn_gate(a_s, b_s, state_ffn_conv, ffn_conv_w, ffn_conv_b, layer, bs)
        xp, xs = _matmul(gp, gs, w_ffn_out, layer, res=xp, ress=xs, tm_target=512, tn_target=512)
        xp, xs = xp[0], xs[0]
        cp_l.append(cp)
        cs_l.append(cs)

    kv_shape_p = (n_sb, bp, tp, heads, HEAD_DIM)
    kv_shape_s = (n_sb, bs, ts, heads, HEAD_DIM)
    return (xp.reshape(bp, tp, d), xs.reshape(bs, ts, d),
            kv_p[0].reshape(kv_shape_p), kv_p[1].reshape(kv_shape_p),
            kv_s[0].reshape(kv_shape_s), kv_s[1].reshape(kv_shape_s),
            jnp.stack(sp_l), jnp.stack(ss_l), jnp.stack(cp_l), jnp.stack(cs_l))
```

```python
import functools

import jax
import jax.numpy as jnp
import numpy as np
from jax import lax
from jax.experimental import pallas as pl
from jax.experimental.pallas import tpu as pltpu

F32 = jnp.float32
BF16 = jnp.bfloat16

HEAD_DIM = 128
SUBLANES = 8
NORM_EPS = 1e-6
FFN_CONV_W = 3
SB_TILE = 256
SB_HEADS_PER_STEP = 4
SB_PAGES_PER_STEP = 4
FFN_ROW_BLOCKS = 8
HG_HEADS_PER_STEP = 4
HG_CHUNK = 64
HG_SUB = 16
VMEM_LIMIT_BYTES = 56 * 1024 * 1024


def _params(*semantics):
    return pltpu.CompilerParams(dimension_semantics=semantics, vmem_limit_bytes=VMEM_LIMIT_BYTES)


def _pick(n, target, quantum=HEAD_DIM):
    best = None
    for d in range(quantum, min(n, target) + 1, quantum):
        if n % d == 0:
            best = d
    return best if best is not None else n


LOG2_E = 1.4426950408889634


def _softplus_log2(z):
    sign_bit = jnp.int32(-2 ** 31)
    neg_abs = lax.bitcast_convert_type(lax.bitcast_convert_type(z, jnp.int32) | sign_bit, F32)
    return jnp.maximum(z, 0.0) + jnp.log2(1.0 + jnp.exp2(neg_abs))


def _split_bf16(x, parts):
    out = []
    r = x
    for _ in range(parts - 1):
        p = r.astype(BF16)
        out.append(p)
        r = r - p.astype(F32)
    out.append(r.astype(BF16))
    return out


def _rmsnorm_kernel(x_ref, g_ref, o_ref):
    x = x_ref[...]
    ms = jnp.mean(x * x, axis=-1, keepdims=True)
    o_ref[...] = (x * lax.rsqrt(ms + NORM_EPS) * g_ref[...]).astype(o_ref.dtype)


def _rmsnorm(x, gains, layer):
    m, d = x.shape
    tm = _pick(m, 512, SUBLANES)
    return pl.pallas_call(
        _rmsnorm_kernel,
        out_shape=jax.ShapeDtypeStruct((m, d), BF16),
        grid=(m // tm,),
        in_specs=[pl.BlockSpec((tm, d), lambda i: (i, 0)),
                  pl.BlockSpec((None, 1, d), lambda i: (layer, 0, 0))],
        out_specs=pl.BlockSpec((tm, d), lambda i: (i, 0)),
        compiler_params=_params("arbitrary"),
        name="rmsnorm",
    )(x, gains[:, None, :])


def _matmul_kernel(*refs, has_res):
    if has_res:
        x_ref, w_ref, r_ref, xs_ref, rs_ref, o_ref, os_ref, wb_ref = refs
    else:
        x_ref, w_ref, xs_ref, o_ref, os_ref, wb_ref = refs

    @pl.when(pl.program_id(1) == 0)
    def _():
        wb_ref[...] = w_ref[...].astype(BF16)
        acc_s = jnp.dot(xs_ref[...].astype(BF16), wb_ref[...], preferred_element_type=F32)
        if has_res:
            acc_s = acc_s + rs_ref[...]
        os_ref[...] = acc_s

    acc = jnp.dot(x_ref[...].astype(BF16), wb_ref[...], preferred_element_type=F32)
    if has_res:
        acc = acc + r_ref[...]
    o_ref[...] = acc.astype(o_ref.dtype)


def _matmul(x, xs, w, layer, *, res=None, ress=None, sections=1, tm_target=1024, tn_target=1024):
    m, k = x.shape
    ms = xs.shape[0]
    n = w.shape[-1]
    ns = n // sections
    tm = _pick(m, tm_target, SUBLANES)
    tn = _pick(ns, tn_target)
    per = ns // tn
    in_specs = [pl.BlockSpec((tm, k), lambda j, i: (i, 0)),
                pl.BlockSpec((None, k, tn), lambda j, i: (layer, 0, j))]
    args = [x, w]
    if res is not None:
        assert sections == 1
        in_specs.append(pl.BlockSpec((tm, tn), lambda j, i: (i, j)))
        args.append(res)
    in_specs.append(pl.BlockSpec((ms, k), lambda j, i: (0, 0)))
    args.append(xs)
    if res is not None:
        in_specs.append(pl.BlockSpec((ms, tn), lambda j, i: (0, j)))
        args.append(ress)
    return pl.pallas_call(
        functools.partial(_matmul_kernel, has_res=res is not None),
        out_shape=(jax.ShapeDtypeStruct((sections, m, ns), F32),
                   jax.ShapeDtypeStruct((sections, ms, ns), F32)),
        grid=(n // tn, m // tm),
        in_specs=in_specs,
        out_specs=(pl.BlockSpec((None, tm, tn), lambda j, i: (j // per, i, j % per)),
                   pl.BlockSpec((None, ms, tn), lambda j, i: (j // per, 0, j % per))),
        scratch_shapes=[pltpu.VMEM((k, tn), BF16)],
        compiler_params=_params("arbitrary", "arbitrary"),
        name="matmul",
    )(*args)


def _kv_heads_kernel(k_ref, v_ref, g_ref, *rest, heads):
    kb_ref, vb_ref, knh_ref, vh_ref = rest[-4:]
    g = g_ref[...]
    tm = k_ref.shape[0]
    for h in range(heads):
        cols = slice(h * HEAD_DIM, (h + 1) * HEAD_DIM)
        x = k_ref[:, cols]
        ms = jnp.mean(x * x, axis=-1, keepdims=True)
        kn = x * lax.rsqrt(ms + NORM_EPS) * g
        v = v_ref[:, cols]
        kb_ref[:, cols] = kn.astype(BF16)
        vb_ref[:, cols] = v.astype(BF16)
        knh_ref[pl.ds(h, tm, stride=heads), :] = kn
        vh_ref[pl.ds(h, tm, stride=heads), :] = v


def _kv_heads(qkv3, gains, layer, n_layers, stacks):
    _, m, d = qkv3.shape
    heads = d // HEAD_DIM
    tm = _pick(m, 256, SUBLANES)
    flat = jax.ShapeDtypeStruct((m, d), BF16)
    by_head = jax.ShapeDtypeStruct((n_layers, m * heads, HEAD_DIM), F32)
    flat_spec = pl.BlockSpec((tm, d), lambda i: (i, 0))
    head_spec = pl.BlockSpec((None, tm * heads, HEAD_DIM), lambda i: (layer, i, 0))
    in_specs = [pl.BlockSpec((None, tm, d), lambda i: (1, i, 0)),
                pl.BlockSpec((None, tm, d), lambda i: (2, i, 0)),
                pl.BlockSpec((None, 1, HEAD_DIM), lambda i: (layer, 0, 0))]
    args = [qkv3, qkv3, gains[:, None, :]]
    aliases = {}
    if stacks is not None:
        in_specs += [pl.BlockSpec(memory_space=pl.ANY)] * 2
        args += list(stacks)
        aliases = {3: 2, 4: 3}
    return pl.pallas_call(
        functools.partial(_kv_heads_kernel, heads=heads),
        out_shape=(flat, flat, by_head, by_head),
        grid=(m // tm,),
        in_specs=in_specs,
        out_specs=(flat_spec, flat_spec, head_spec, head_spec),
        input_output_aliases=aliases,
        compiler_params=_params("arbitrary"),
        name="kv_heads",
    )(*args)


SPLIT_PARTS = 2


def _split_lanes(x):
    return jnp.concatenate(_split_bf16(x, SPLIT_PARTS), axis=1)


def _suffix_ones(n):
    r = lax.broadcasted_iota(jnp.int32, (SPLIT_PARTS * n, n), 0) % n
    c = lax.broadcasted_iota(jnp.int32, (SPLIT_PARTS * n, n), 1)
    return jnp.where(r > c, 1.0, 0.0).astype(BF16)


def _nt_dot(a, b):
    return lax.dot_general(a, b, (((1,), (1,)), ((), ())), preferred_element_type=F32)


def _sb_log_keep(z, mask):
    sp = _softplus_log2(z)
    lk = -sp if mask is None else jnp.where(mask, -sp, 0.0)
    return lk, z - sp


def _sb_tail(lk, carry, u):
    return carry + jnp.dot(_split_lanes(lk), u, preferred_element_type=F32)


def _normed_query(q, gain):
    ms = jnp.mean(q * q, axis=-1, keepdims=True)
    return q * lax.rsqrt(ms + NORM_EPS) * gain * (HEAD_DIM ** -0.5 * LOG2_E)


def _sb_prompt_kernel(bias_ref, q_ref, k_ref, v_ref, qg_ref, o_ref, qb_ref, acc_ref, carry_ref,
                      *, group):
    hg = pl.program_id(1)
    i = pl.program_id(2)
    tile = q_ref.shape[0]
    u = _suffix_ones(tile)
    row = lax.broadcasted_iota(jnp.int32, (tile, tile), 0)
    col = lax.broadcasted_iota(jnp.int32, (tile, tile), 1)
    for g in range(group):
        cols = slice(g * HEAD_DIM, (g + 1) * HEAD_DIM)
        qb_ref[:, cols] = _normed_query(q_ref[:, cols], qg_ref[...]).astype(BF16)
    acc_ref[...] = jnp.zeros_like(acc_ref)
    carry_ref[...] = jnp.zeros_like(carry_ref)

    def key_tile(j, mask):
        keys = pl.ds(pl.multiple_of(j * tile, tile), tile)
        heads = [slice(g * HEAD_DIM, (g + 1) * HEAD_DIM) for g in range(group)]
        zs = [_nt_dot(qb_ref[:, cols], k_ref[keys, cols]) + bias_ref[hg * group + g] * LOG2_E
              for g, cols in enumerate(heads)]
        keeps = [_sb_log_keep(z, mask) for z in zs]
        tails = [_sb_tail(lk, carry_ref[g], u) for g, (lk, _) in enumerate(keeps)]
        for g, cols in enumerate(heads):
            lk, log_beta = keeps[g]
            a = jnp.exp2(log_beta + tails[g])
            if mask is not None:
                a = jnp.where(mask, a, 0.0)
            acc_ref[:, cols] += jnp.dot(a.astype(BF16), v_ref[keys, cols],
                                        preferred_element_type=F32)
            carry_ref[g] += jnp.sum(lk, axis=-1, keepdims=True)

    key_tile(i, col < row)

    def body(step, _):
        key_tile(i - 1 - step, None)
        return 0

    lax.fori_loop(0, i, body, 0)
    o_ref[...] = acc_ref[...].astype(o_ref.dtype)


def _sb_prompt(qkv3, kb, vb, q_gain, bias, layer, batch):
    _, m, d = qkv3.shape
    t = m // batch
    heads = d // HEAD_DIM
    tile = _pick(t, SB_TILE)
    nq = t // tile
    group = SB_HEADS_PER_STEP if heads % SB_HEADS_PER_STEP == 0 else 1
    width = group * HEAD_DIM
    kv_spec = pl.BlockSpec((t, width), lambda b, h, i: (b, h))
    return pl.pallas_call(
        functools.partial(_sb_prompt_kernel, group=group),
        out_shape=jax.ShapeDtypeStruct((m, d), BF16),
        grid=(batch, heads // group, nq),
        in_specs=[pl.BlockSpec(memory_space=pltpu.SMEM),
                  pl.BlockSpec((None, tile, width), lambda b, h, i: (0, b * nq + i, h)),
                  kv_spec,
                  kv_spec,
                  pl.BlockSpec((None, 1, HEAD_DIM), lambda b, h, i: (layer, 0, 0))],
        out_specs=pl.BlockSpec((tile, width), lambda b, h, i: (b * nq + i, h)),
        scratch_shapes=[pltpu.VMEM((tile, width), BF16),
                        pltpu.VMEM((tile, width), F32),
                        pltpu.VMEM((group, tile, 1), F32)],
        compiler_params=_params("arbitrary", "arbitrary", "arbitrary"),
        name="sb_prompt",
    )(bias[layer], qkv3, kb, vb, q_gain[:, None, :])


def _sb_sample_kernel(pt_ref, q_ref, kn_ref, vn_ref, qg_ref, bias_ref, hmask_ref, wsuf_ref, *rest,
                      heads, tq, page, pages_per_step):
    del pt_ref
    kc_refs = rest[:pages_per_step]
    vc_refs = rest[pages_per_step:2 * pages_per_step]
    o_ref, qall_ref, acc_ref, carry_ref, kpad_ref, vpad_ref = rest[2 * pages_per_step:]
    p = pl.program_id(1)
    lanes = page * heads
    n_blk = lanes // HEAD_DIM
    n_grp = heads * tq // SUBLANES

    def tiles(kv_refs, valid):
        logits = []
        for k_ref, _ in kv_refs:
            f = _nt_dot(qall_ref[...].astype(BF16), k_ref[...].astype(BF16)) * hmask_ref[...]
            zs = f[0:SUBLANES]
            for g in range(1, n_grp):
                zs = zs + f[g * SUBLANES:(g + 1) * SUBLANES]
            z = zs
            for k in range(1, SUBLANES // tq):
                z = z + pltpu.roll(zs, k * tq, axis=0)
            z = z + jnp.concatenate([bias_ref[...] * LOG2_E] * n_blk, axis=1)
            logits.append(_sb_log_keep(z, valid))
        sums = []
        for lk, _ in logits:
            blocks = jnp.concatenate(
                [lk[:, b * HEAD_DIM:(b + 1) * HEAD_DIM] for b in range(n_blk)], axis=0)
            sums.append(jnp.dot(_split_lanes(blocks), wsuf_ref[...], preferred_element_type=F32))
        run = carry_ref[...]
        out = jnp.zeros(acc_ref.shape, F32)
        for (_, log_beta), res, (_, v_ref) in zip(logits, sums, kv_refs):
            tails = [None] * n_blk
            for b in reversed(range(n_blk)):
                blk = res[b * SUBLANES:(b + 1) * SUBLANES]
                tails[b] = blk[:, :HEAD_DIM] + run
                run = run + blk[:, HEAD_DIM:]
            a = jnp.exp2(log_beta + jnp.concatenate(tails, axis=1))
            if valid is not None:
                a = jnp.where(valid, a, 0.0)
            a_heads = (jnp.concatenate([a] * n_grp, axis=0) * hmask_ref[...]).astype(BF16)
            out = out + jnp.dot(a_heads, v_ref[...].astype(BF16), preferred_element_type=F32)
        carry_ref[...] = run
        acc_ref[...] += out

    @pl.when(p == 0)
    def _new_tokens():
        for h in range(heads):
            cols = slice(h * HEAD_DIM, (h + 1) * HEAD_DIM)
            qall_ref[h * tq:(h + 1) * tq, :] = _normed_query(q_ref[:, cols], qg_ref[...])
        kpad_ref[...] = jnp.zeros_like(kpad_ref)
        vpad_ref[...] = jnp.zeros_like(vpad_ref)
        kpad_ref[0:tq * heads, :] = kn_ref[...]
        vpad_ref[0:tq * heads, :] = vn_ref[...]
        acc_ref[...] = jnp.zeros_like(acc_ref)
        carry_ref[...] = jnp.zeros_like(carry_ref)
        query = lax.broadcasted_iota(jnp.int32, (SUBLANES, lanes), 0) % tq
        position = lax.broadcasted_iota(jnp.int32, (SUBLANES, lanes), 1) // heads
        tiles([(kpad_ref, vpad_ref)], position < query)

    @pl.when(p > 0)
    def _past_page():
        tiles(list(zip(kc_refs, vc_refs)), None)

    @pl.when(p == pl.num_programs(1) - 1)
    def _finish():
        for h in range(heads):
            o_ref[:, h * HEAD_DIM:(h + 1) * HEAD_DIM] = acc_ref[h * tq:(h + 1) * tq, :]


def _sb_sample(qkv3, kn, vh, q_gain, bias, cache_k, cache_v, page_table, layer, batch):
    _, m, d = qkv3.shape
    tq = m // batch
    heads = d // HEAD_DIM
    assert SUBLANES % tq == 0 and (heads * tq) % SUBLANES == 0 and HEAD_DIM % heads == 0
    rows = heads * tq
    n_pages = page_table.shape[1]
    lanes = cache_k.shape[2]
    page = lanes // heads
    qkv4 = qkv3.reshape(3, batch, tq, d)
    lane_head = np.arange(lanes) % heads
    bias_tile = jnp.broadcast_to(jnp.tile(bias[layer], HEAD_DIM // heads)[None, :],
                                 (SUBLANES, HEAD_DIM)).astype(F32)
    hmask = jnp.asarray(lane_head[None, :] == (np.arange(rows) // tq)[:, None], F32)
    l = np.arange(HEAD_DIM)
    same_head = (l[:, None] % heads) == (l[None, :] % heads)
    later = (l[:, None] // heads) > (l[None, :] // heads)
    wsuf = np.concatenate([same_head & later, same_head], axis=1)
    wsuf = jnp.asarray(np.concatenate([wsuf] * SPLIT_PARTS, axis=0), BF16)

    pps = SB_PAGES_PER_STEP if n_pages % SB_PAGES_PER_STEP == 0 else 1

    def page_index(j):
        return lambda b, p, pt: (layer, pt[b, n_pages - 1 - ((jnp.maximum(p, 1) - 1) * pps + j)], 0, 0)

    page_specs = [pl.BlockSpec((None, None, lanes, HEAD_DIM), page_index(j)) for j in range(pps)]
    fixed = lambda shape: pl.BlockSpec(shape, lambda b, p, pt: (0,) * len(shape))
    new_spec = pl.BlockSpec((None, None, tq * heads, HEAD_DIM), lambda b, p, pt: (layer, b, 0, 0))
    stack_shape = (kn.shape[0], batch, tq * heads, HEAD_DIM)
    grid_spec = pltpu.PrefetchScalarGridSpec(
        num_scalar_prefetch=1,
        grid=(batch, n_pages // pps + 1),
        in_specs=[pl.BlockSpec((None, None, tq, d), lambda b, p, pt: (0, b, 0, 0)),
                  new_spec,
                  new_spec,
                  pl.BlockSpec((None, 1, HEAD_DIM), lambda b, p, pt: (layer, 0, 0)),
                  fixed(bias_tile.shape),
                  fixed(hmask.shape),
                  fixed(wsuf.shape)] + page_specs + page_specs,
        out_specs=pl.BlockSpec((None, tq, d), lambda b, p, pt: (b, 0, 0)),
        scratch_shapes=[pltpu.VMEM((rows, HEAD_DIM), F32),
                        pltpu.VMEM((rows, HEAD_DIM), F32),
                        pltpu.VMEM((SUBLANES, HEAD_DIM), F32),
                        pltpu.VMEM((lanes, HEAD_DIM), F32),
                        pltpu.VMEM((lanes, HEAD_DIM), F32)])
    out = pl.pallas_call(
        functools.partial(_sb_sample_kernel, heads=heads, tq=tq, page=page, pages_per_step=pps),
        out_shape=jax.ShapeDtypeStruct((batch, tq, d), F32),
        grid_spec=grid_spec,
        compiler_params=_params("arbitrary", "arbitrary"),
        name="sb_sample",
    )(page_table, qkv4, kn.reshape(stack_shape), vh.reshape(stack_shape),
      q_gain[:, None, :], bias_tile, hmask, wsuf, *([cache_k] * pps), *([cache_v] * pps))
    return out.reshape(m, d)


def _lower_bounds_kernel(x_ref, lb_ref):
    x = x_ref[...]
    e = jnp.exp(x - jnp.max(x, axis=0, keepdims=True))
    soft = e / jnp.sum(e, axis=0, keepdims=True)
    depth = x.shape[0]
    run = jnp.zeros_like(soft[0:1])
    for layer in range(depth):
        run = run + soft[layer:layer + 1]
        lb = run - soft[0:1]
        lb_ref[layer, 0:1, :] = lb
        lb_ref[layer, 1:2, :] = jnp.log(lb)
        lb_ref[layer, 2:3, :] = jnp.log1p(-lb)
        lb_ref[layer, 3:4, :] = 1.0 - lb
        lb_ref[layer, 4:8, :] = jnp.zeros((4, x.shape[1]), F32)


def _lower_bounds(hg_lower_bounds):
    depth, d = hg_lower_bounds.shape
    return pl.pallas_call(
        _lower_bounds_kernel,
        out_shape=jax.ShapeDtypeStruct((depth, SUBLANES, d), F32),
        name="hg_lower_bounds",
    )(hg_lower_bounds.astype(F32))


def _hg_gates(f, lbp):
    log_lb, log_1m, one_m = lbp[1:2], lbp[2:3], lbp[3:4]
    log_sig = jnp.minimum(f, 0.0) - jnp.log(1.0 + jnp.exp(-jnp.abs(f)))
    b = log_1m + log_sig
    g = jnp.maximum(log_lb, b) + jnp.log(1.0 + jnp.exp(-jnp.abs(log_lb - b)))
    key = one_m * (1.0 / (1.0 + jnp.exp(f)))
    return g, key


def _hg_output(o, gate, gain):
    ms = jnp.mean(o * o, axis=-1, keepdims=True)
    on = o * lax.rsqrt(ms + NORM_EPS) * gain
    return on * (gate * (1.0 / (1.0 + jnp.exp(-gate))))


def _hg_prompt_kernel(q_ref, f_ref, i_ref, g_ref, lbp_ref, gain_ref, o_ref, s_ref,
                      st_ref, gc_ref, *, chunk, sub, group):
    t = q_ref.shape[0]
    n_sub = chunk // sub
    n_chunks = t // chunk
    gain = gain_ref[...]
    r = lax.broadcasted_iota(jnp.int32, (chunk, chunk), 0)
    c = lax.broadcasted_iota(jnp.int32, (chunk, chunk), 1)
    lower = jnp.where(c <= r, 1.0, 0.0).astype(BF16)
    crow = lax.broadcasted_iota(jnp.int32, (chunk, HEAD_DIM), 0)
    srow = lax.broadcasted_iota(jnp.int32, (sub, HEAD_DIM), 0)
    own = (lax.broadcasted_iota(jnp.int32, (chunk, chunk * sub), 1) // sub
           == lax.broadcasted_iota(jnp.int32, (chunk, chunk * sub), 0))
    st_ref[...] = jnp.zeros_like(st_ref)

    def head_cols(hi):
        return slice(hi * HEAD_DIM, (hi + 1) * HEAD_DIM)

    def decays(ci):
        rows = pl.ds(pl.multiple_of(ci * chunk, chunk), chunk)
        gates = [_hg_gates(f_ref[rows, head_cols(hi)], lbp_ref[:, head_cols(hi)])
                 for hi in range(group)]
        parts = [p for glog, _ in gates for p in _split_bf16(glog * LOG2_E, 3)]
        sums = jnp.dot(lower, jnp.concatenate(parts, axis=1), preferred_element_type=F32)
        out = []
        for hi, (_, kk) in enumerate(gates):
            gc = sums[:, head_cols(3 * hi)] + sums[:, head_cols(3 * hi + 1)] + sums[:, head_cols(3 * hi + 2)]
            out.append((gc, kk))
        return tuple(out)

    def scores(r0, hi, gc, kk):
        rows = pl.ds(r0, chunk)
        gc_ref[hi] = gc
        q = q_ref[rows, head_cols(hi)]
        vb = i_ref[rows, head_cols(hi)].astype(BF16)
        st = st_ref[hi]
        inter = _nt_dot((q * jnp.exp2(gc)).astype(BF16), st.astype(BF16))
        g_last = gc[chunk - 1:chunk, :]
        offs = [jnp.zeros((sub, chunk), F32)]
        stacked = []
        for si in range(n_sub):
            base = si * sub
            blk = slice(base, base + sub)
            g_s, k_s = gc[blk], kk[blk]
            if si > 0:
                g_start = gc[base - 1:base, :]
                k_prev = jnp.where(crow < base, kk * jnp.exp2(g_start - gc), 0.0)
                q_in = q[blk] * jnp.exp2(g_s - g_start)
                offs.append(_nt_dot(q_in.astype(BF16), k_prev.astype(BF16)))
            for ti in range(sub):
                g_t = gc_ref[hi, base + ti:base + ti + 1, :]
                k_t = jnp.where(srow <= ti, k_s * jnp.exp2(g_t - g_s), 0.0)
                stacked.append(k_t.astype(BF16))
        diag = _nt_dot(q.astype(BF16), jnp.concatenate(stacked, axis=0))
        k_dec = kk * jnp.exp2(g_last - gc)
        st_ref[hi] = st * jnp.exp2(g_last) + lax.dot_general(
            vb, k_dec.astype(BF16), (((0,), (0,)), ((), ())), preferred_element_type=F32)
        return vb, inter, jnp.concatenate(offs, axis=0), diag

    def outputs(r0, hi, vb, inter, off, diag):
        rows = pl.ds(r0, chunk)
        v_rep = jnp.concatenate(
            [vb[si * sub:(si + 1) * sub] for si in range(n_sub) for _ in range(sub)], axis=0)
        o = (inter + jnp.dot(off.astype(BF16), vb, preferred_element_type=F32)
             + jnp.dot(jnp.where(own, diag, 0.0).astype(BF16), v_rep, preferred_element_type=F32))
        o_ref[rows, head_cols(hi)] = _hg_output(o, g_ref[rows, head_cols(hi)], gain).astype(o_ref.dtype)

    def body(ci, current):
        r0 = pl.multiple_of(ci * chunk, chunk)
        following = decays(jnp.minimum(ci + 1, n_chunks - 1))
        staged = [scores(r0, hi, *current[hi]) for hi in range(group)]
        for hi in range(group):
            outputs(r0, hi, *staged[hi])
        return following

    lax.fori_loop(0, n_chunks, body, decays(0))
    for hi in range(group):
        s_ref[hi] = st_ref[hi].T


def _hg_prompt(proj, lbp, out_gain, layer, hg_index, batch):
    m, d4 = proj.shape
    d = d4 // 4
    heads = d // HEAD_DIM
    t = m // batch
    chunk = _pick(t, HG_CHUNK, HG_SUB)
    sub = HG_SUB
    group = HG_HEADS_PER_STEP if heads % HG_HEADS_PER_STEP == 0 else 1
    width = group * HEAD_DIM
    ng = heads // group
    col = lambda sec: (lambda b, h: (b, sec * ng + h))
    return pl.pallas_call(
        functools.partial(_hg_prompt_kernel, chunk=chunk, sub=sub, group=group),
        out_shape=(jax.ShapeDtypeStruct((m, d), BF16),
                   jax.ShapeDtypeStruct((batch, heads, HEAD_DIM, HEAD_DIM), F32)),
        grid=(batch, ng),
        in_specs=[pl.BlockSpec((t, width), col(0)),
                  pl.BlockSpec((t, width), col(1)),
                  pl.BlockSpec((t, width), col(2)),
                  pl.BlockSpec((t, width), col(3)),
                  pl.BlockSpec((None, SUBLANES, width), lambda b, h: (layer, 0, h)),
                  pl.BlockSpec((None, 1, HEAD_DIM), lambda b, h: (hg_index, 0, 0))],
        out_specs=(pl.BlockSpec((t, width), lambda b, h: (b, h)),
                   pl.BlockSpec((None, group, HEAD_DIM, HEAD_DIM), lambda b, h: (b, h, 0, 0))),
        scratch_shapes=[pltpu.VMEM((group, HEAD_DIM, HEAD_DIM), F32),
                        pltpu.VMEM((group, chunk, HEAD_DIM), F32)],
        compiler_params=_params("arbitrary", "arbitrary"),
        name="hg_prompt",
    )(proj, proj, proj, proj, lbp, out_gain[:, None, :])


def _hg_sample_kernel(q_ref, f_ref, i_ref, g_ref, lbp_ref, gain_ref, s0_ref, o_ref, s_ref, pad_ref,
                      *, heads):
    tq = q_ref.shape[0]
    pad_ref[...] = jnp.zeros_like(pad_ref)

    def columns(slot, x):
        pad_ref[slot, 0:tq, :] = x
        return pad_ref[slot].T

    for h in range(heads):
        cols = slice(h * HEAD_DIM, (h + 1) * HEAD_DIM)
        glog, kk = _hg_gates(f_ref[:, cols], lbp_ref[:, cols])
        q_c = columns(3 * h, q_ref[:, cols])
        f_c = columns(3 * h + 1, jnp.exp(glog))
        k_c = columns(3 * h + 2, kk)
        v = i_ref[:, cols]
        s = s0_ref[h]
        for ti in range(tq):
            s = f_c[:, ti:ti + 1] * s + k_c[:, ti:ti + 1] * v[ti:ti + 1, :]
            o = jnp.sum(q_c[:, ti:ti + 1] * s, axis=0, keepdims=True)
            o_ref[ti:ti + 1, cols] = _hg_output(o, g_ref[ti:ti + 1, cols], gain_ref[...])
        s_ref[h] = s


def _hg_sample(proj, lbp, out_gain, state, layer, hg_index, batch):
    m, d4 = proj.shape
    d = d4 // 4
    heads = d // HEAD_DIM
    tq = m // batch
    proj3 = proj.reshape(batch, tq, d4)
    col = lambda sec: (lambda b: (b, 0, sec))
    state_block = (None, heads, HEAD_DIM, HEAD_DIM)
    out, s_new = pl.pallas_call(
        functools.partial(_hg_sample_kernel, heads=heads),
        out_shape=(jax.ShapeDtypeStruct((batch, tq, d), F32),
                   jax.ShapeDtypeStruct((batch, heads, HEAD_DIM, HEAD_DIM), F32)),
        grid=(batch,),
        in_specs=[pl.BlockSpec((None, tq, d), col(0)),
                  pl.BlockSpec((None, tq, d), col(1)),
                  pl.BlockSpec((None, tq, d), col(2)),
                  pl.BlockSpec((None, tq, d), col(3)),
                  pl.BlockSpec((None, SUBLANES, d), lambda b: (layer, 0, 0)),
                  pl.BlockSpec((None, 1, HEAD_DIM), lambda b: (hg_index, 0, 0)),
                  pl.BlockSpec((None,) + state_block, lambda b: (hg_index, b, 0, 0, 0))],
        out_specs=(pl.BlockSpec((None, tq, d), lambda b: (b, 0, 0)),
                   pl.BlockSpec(state_block, lambda b: (b, 0, 0, 0))),
        scratch_shapes=[pltpu.VMEM((3 * heads, HEAD_DIM, HEAD_DIM), F32)],
        compiler_params=_params("arbitrary"),
        name="hg_sample",
    )(proj3, proj3, proj3, proj3, lbp, out_gain[:, None, :], state)
    return out.reshape(m, d), s_new


def _ffn_gate_kernel(a_ref, b_ref, buf_ref, w_ref, cb_ref, o_ref, st_ref, ext_ref):
    t = a_ref.shape[0]
    ext_ref[CONV_LEAD:SUBLANES, :] = buf_ref[...]
    o_ref[...] = _conv_gate(ext_ref, a_ref[...], b_ref[...], w_ref, cb_ref).astype(o_ref.dtype)
    st_ref[...] = ext_ref[SUBLANES + t - (FFN_CONV_W - 1):SUBLANES + t, :]


CONV_LEAD = SUBLANES - (FFN_CONV_W - 1)


def _conv_gate(ext_ref, a, b, w_ref, cb_ref):
    t = a.shape[0]
    lead = CONV_LEAD
    ext_ref[SUBLANES:SUBLANES + t, :] = a
    c = cb_ref[...]
    for j in range(FFN_CONV_W - 1):
        c = c + w_ref[j:j + 1, :] * ext_ref[lead + j:lead + j + t, :]
    c = c + w_ref[FFN_CONV_W - 1:FFN_CONV_W, :] * a
    return c * (1.0 / (1.0 + jnp.exp(-c))) * b


def _ffn_in_kernel(x_ref, wa_ref, wb_ref, xs_ref, cw_ref, cb_ref, o_ref, st_ref, as_ref, bs_ref,
                   wa16_ref, wb16_ref, hist_ref, *, tiles_per_seq, row_blocks):
    i = pl.program_id(1)
    tm = x_ref.shape[0]

    @pl.when(i == 0)
    def _():
        wa16_ref[...] = wa_ref[...].astype(BF16)
        wb16_ref[...] = wb_ref[...].astype(BF16)
        xs = xs_ref[...].astype(BF16)
        as_ref[...] = jnp.dot(xs, wa16_ref[...], preferred_element_type=F32)
        bs_ref[...] = jnp.dot(xs, wb16_ref[...], preferred_element_type=F32)

    @pl.when(i % tiles_per_seq == 0)
    def _():
        hist_ref[...] = jnp.zeros_like(hist_ref)

    rb = tm // row_blocks
    row = lax.broadcasted_iota(jnp.int32, (SUBLANES, o_ref.shape[1]), 0)

    def products(r):
        x = x_ref[r * rb:(r + 1) * rb, :]
        return (jnp.dot(x, wa16_ref[...], preferred_element_type=F32),
                jnp.dot(x, wb16_ref[...], preferred_element_type=F32))

    def shifted(a, first_rows):
        k = len(first_rows)
        moved = pltpu.roll(a, k, axis=0)
        top = moved[0:SUBLANES]
        for j, r in enumerate(first_rows):
            top = jnp.where(row == j, r, top)
        return jnp.concatenate([top, moved[SUBLANES:]], axis=0)

    def gate(r, a, b, before):
        c = cb_ref[...]
        c = c + cw_ref[0:1, :] * shifted(a, [before[0:1], before[1:2]])
        c = c + cw_ref[1:2, :] * shifted(a, [before[1:2]])
        c = c + cw_ref[2:3, :] * a
        o_ref[r * rb:(r + 1) * rb, :] = (c * (1.0 / (1.0 + jnp.exp(-c))) * b).astype(o_ref.dtype)

    before = hist_ref[...]
    pending = products(0)
    for r in range(1, row_blocks):
        issued = products(r)
        gate(r - 1, *pending, before)
        before = pending[0][rb - (FFN_CONV_W - 1):rb, :]
        pending = issued
    gate(row_blocks - 1, *pending, before)
    last = pending[0][rb - (FFN_CONV_W - 1):rb, :]
    hist_ref[...] = last
    st_ref[...] = last


def _ffn_in(x, xs, w_in, conv_w, conv_b, layer, batch):
    m, k = x.shape
    ms = xs.shape[0]
    f = w_in.shape[-1] // 2
    t = m // batch
    tm = _pick(t, 1024, SUBLANES)
    tps = t // tm
    tf = _pick(f, 512)
    nf = f // tf
    return pl.pallas_call(
        functools.partial(_ffn_in_kernel, tiles_per_seq=tps,
                          row_blocks=FFN_ROW_BLOCKS if tm % (FFN_ROW_BLOCKS * SUBLANES) == 0 else 1),
        out_shape=(jax.ShapeDtypeStruct((m, f), BF16),
                   jax.ShapeDtypeStruct((batch, FFN_CONV_W - 1, f), F32),
                   jax.ShapeDtypeStruct((ms, f), F32),
                   jax.ShapeDtypeStruct((ms, f), F32)),
        grid=(nf, m // tm),
        in_specs=[pl.BlockSpec((tm, k), lambda j, i: (i, 0)),
                  pl.BlockSpec((None, k, tf), lambda j, i: (layer, 0, j)),
                  pl.BlockSpec((None, k, tf), lambda j, i: (layer, 0, nf + j)),
                  pl.BlockSpec((ms, k), lambda j, i: (0, 0)),
                  pl.BlockSpec((None, FFN_CONV_W, tf), lambda j, i: (layer, 0, j)),
                  pl.BlockSpec((None, 1, tf), lambda j, i: (layer, 0, j))],
        out_specs=(pl.BlockSpec((tm, tf), lambda j, i: (i, j)),
                   pl.BlockSpec((None, FFN_CONV_W - 1, tf), lambda j, i: (i // tps, 0, j)),
                   pl.BlockSpec((ms, tf), lambda j, i: (0, j)),
                   pl.BlockSpec((ms, tf), lambda j, i: (0, j))),
        scratch_shapes=[pltpu.VMEM((k, tf), BF16),
                        pltpu.VMEM((k, tf), BF16),
                        pltpu.VMEM((FFN_CONV_W - 1, tf), F32)],
        compiler_params=_params("arbitrary", "arbitrary"),
        name="ffn_in",
    )(x, w_in, w_in, xs, conv_w, conv_b[:, None, :])


def _ffn_gate(a, b, buf, conv_w, conv_b, layer, batch):
    m, f = a.shape
    t = m // batch
    row_spec = pl.BlockSpec((None, t, f), lambda i: (i, 0, 0))
    hist_spec = pl.BlockSpec((None, FFN_CONV_W - 1, f), lambda i: (i, 0, 0))
    out, st = pl.pallas_call(
        _ffn_gate_kernel,
        out_shape=(jax.ShapeDtypeStruct((batch, t, f), F32),
                   jax.ShapeDtypeStruct((batch, FFN_CONV_W - 1, f), F32)),
        grid=(batch,),
        in_specs=[row_spec,
                  row_spec,
                  pl.BlockSpec((None, None, FFN_CONV_W - 1, f), lambda i: (layer, i, 0, 0)),
                  pl.BlockSpec((None, FFN_CONV_W, f), lambda i: (layer, 0, 0)),
                  pl.BlockSpec((None, 1, f), lambda i: (layer, 0, 0))],
        out_specs=(row_spec, hist_spec),
        scratch_shapes=[pltpu.VMEM((t + SUBLANES, f), F32)],
        compiler_params=_params("arbitrary"),
        name="ffn_gate",
    )(a.reshape(batch, t, f), b.reshape(batch, t, f), buf, conv_w, conv_b[:, None, :])
    return out.reshape(m, f), st


def kernel(x_prompt, x_sample, cache_sb_k, cache_sb_v, page_table, state_hgrn, state_ffn_conv,
           norm_mixer, norm_ffn, w_sb_qkv, sb_q_gain, sb_k_gain, sb_logit_bias, w_sb_o,
           w_hg_in, hg_lower_bounds, hg_out_gain, w_hg_o,
           w_ffn_in, ffn_conv_w, ffn_conv_b, w_ffn_out):
    bp, tp, d = x_prompt.shape
    bs, ts, _ = x_sample.shape
    depth = norm_mixer.shape[0]
    heads = d // HEAD_DIM
    n_mixers = 2
    n_sb, pool, page = cache_sb_k.shape[:3]
    cache_k = cache_sb_k.reshape(n_sb, pool, page * heads, HEAD_DIM)
    cache_v = cache_sb_v.reshape(n_sb, pool, page * heads, HEAD_DIM)
    lbp = _lower_bounds(hg_lower_bounds)

    xp = x_prompt.reshape(bp * tp, d)
    xs = x_sample.reshape(bs * ts, d)
    kv_p = kv_s = None
    sp_l, ss_l, cp_l, cs_l = [], [], [], []
    for layer in range(depth):
        j = layer // n_mixers
        hp = _rmsnorm(xp, norm_mixer, layer)
        hs = _rmsnorm(xs, norm_mixer, layer)
        if layer % n_mixers == 0:
            qkv_p, qkv_s = _matmul(hp, hs, w_sb_qkv, j, sections=3)
            kb_p, vb_p, *kv_p = _kv_heads(qkv_p, sb_k_gain, j, n_sb, kv_p)
            _, _, *kv_s = _kv_heads(qkv_s, sb_k_gain, j, n_sb, kv_s)
            op = _sb_prompt(qkv_p, kb_p, vb_p, sb_q_gain, sb_logit_bias, j, bp)
            os_ = _sb_sample(qkv_s, kv_s[0], kv_s[1], sb_q_gain, sb_logit_bias, cache_k, cache_v,
                             page_table, j, bs)
            xp, xs = _matmul(op, os_, w_sb_o, j, res=xp, ress=xs)
        else:
            proj_p, proj_s = _matmul(hp, hs, w_hg_in, j)
            op, sp = _hg_prompt(proj_p[0], lbp, hg_out_gain, layer, j, bp)
            os_, ss = _hg_sample(proj_s[0], lbp, hg_out_gain, state_hgrn, layer, j, bs)
            xp, xs = _matmul(op, os_, w_hg_o, j, res=xp, ress=xs)
            sp_l.append(sp)
            ss_l.append(ss)
        xp, xs = xp[0], xs[0]
        hp = _rmsnorm(xp, norm_ffn, layer)
        hs = _rmsnorm(xs, norm_ffn, layer)
        gp, cp, a_s, b_s = _ffn_in(hp, hs, w_ffn_in, ffn_conv_w, ffn_conv_b, layer, bp)
        gs, cs = _ffn_gate(a_s, b_s, state_ffn_conv, ffn_conv_w, ffn_conv_b, layer, bs)
        xp, xs = _matmul(gp, gs, w_ffn_out, layer, res=xp, ress=xs, tm_target=512, tn_target=512)
        xp, xs = xp[0], xs[0]
        cp_l.append(cp)
        cs_l.append(cs)

    kv_shape_p = (n_sb, bp, tp, heads, HEAD_DIM)
    kv_shape_s = (n_sb, bs, ts, heads, HEAD_DIM)
    return (xp.reshape(bp, tp, d), xs.reshape(bs, ts, d),
            kv_p[0].reshape(kv_shape_p), kv_p[1].reshape(kv_shape_p),
            kv_s[0].reshape(kv_shape_s), kv_s[1].reshape(kv_shape_s),
            jnp.stack(sp_l), jnp.stack(ss_l), jnp.stack(cp_l), jnp.stack(cs_l))
```

```python
import functools

import jax
import jax.numpy as jnp
import numpy as np
from jax import lax
from jax.experimental import pallas as pl
from jax.experimental.pallas import tpu as pltpu

F32 = jnp.float32
BF16 = jnp.bfloat16

HEAD_DIM = 128
SUBLANES = 8
NORM_EPS = 1e-6
FFN_CONV_W = 3
SB_TILE = 256
SB_HEADS_PER_STEP = 4
SB_PAGES_PER_STEP = 4
FFN_ROW_BLOCKS = 8
HG_HEADS_PER_STEP = 4
HG_CHUNK = 64
HG_SUB = 16
VMEM_LIMIT_BYTES = 56 * 1024 * 1024


def _params(*semantics):
    return pltpu.CompilerParams(dimension_semantics=semantics, vmem_limit_bytes=VMEM_LIMIT_BYTES)


def _pick(n, target, quantum=HEAD_DIM):
    best = None
    for d in range(quantum, min(n, target) + 1, quantum):
        if n % d == 0:
            best = d
    return best if best is not None else n


LOG2_E = 1.4426950408889634


def _softplus_log2(z):
    sign_bit = jnp.int32(-2 ** 31)
    neg_abs = lax.bitcast_convert_type(lax.bitcast_convert_type(z, jnp.int32) | sign_bit, F32)
    return jnp.maximum(z, 0.0) + jnp.log2(1.0 + jnp.exp2(neg_abs))


def _split_bf16(x, parts):
    out = []
    r = x
    for _ in range(parts - 1):
        p = r.astype(BF16)
        out.append(p)
        r = r - p.astype(F32)
    out.append(r.astype(BF16))
    return out


def _rmsnorm_kernel(x_ref, g_ref, o_ref):
    x = x_ref[...]
    ms = jnp.mean(x * x, axis=-1, keepdims=True)
    o_ref[...] = (x * lax.rsqrt(ms + NORM_EPS) * g_ref[...]).astype(o_ref.dtype)


def _rmsnorm(x, gains, layer):
    m, d = x.shape
    tm = _pick(m, 512, SUBLANES)
    return pl.pallas_call(
        _rmsnorm_kernel,
        out_shape=jax.ShapeDtypeStruct((m, d), BF16),
        grid=(m // tm,),
        in_specs=[pl.BlockSpec((tm, d), lambda i: (i, 0)),
                  pl.BlockSpec((None, 1, d), lambda i: (layer, 0, 0))],
        out_specs=pl.BlockSpec((tm, d), lambda i: (i, 0)),
        compiler_params=_params("arbitrary"),
        name="rmsnorm",
    )(x, gains[:, None, :])


def _matmul_kernel(*refs, has_res):
    if has_res:
        x_ref, w_ref, r_ref, xs_ref, rs_ref, o_ref, os_ref, wb_ref = refs
    else:
        x_ref, w_ref, xs_ref, o_ref, os_ref, wb_ref = refs

    @pl.when(pl.program_id(1) == 0)
    def _():
        wb_ref[...] = w_ref[...].astype(BF16)
        acc_s = jnp.dot(xs_ref[...].astype(BF16), wb_ref[...], preferred_element_type=F32)
        if has_res:
            acc_s = acc_s + rs_ref[...]
        os_ref[...] = acc_s

    acc = jnp.dot(x_ref[...].astype(BF16), wb_ref[...], preferred_element_type=F32)
    if has_res:
        acc = acc + r_ref[...]
    o_ref[...] = acc.astype(o_ref.dtype)


def _matmul(x, xs, w, layer, *, res=None, ress=None, sections=1, tm_target=1024, tn_target=1024):
    m, k = x.shape
    ms = xs.shape[0]
    n = w.shape[-1]
    ns = n // sections
    tm = _pick(m, tm_target, SUBLANES)
    tn = _pick(ns, tn_target)
    per = ns // tn
    in_specs = [pl.BlockSpec((tm, k), lambda j, i: (i, 0)),
                pl.BlockSpec((None, k, tn), lambda j, i: (layer, 0, j))]
    args = [x, w]
    if res is not None:
        assert sections == 1
        in_specs.append(pl.BlockSpec((tm, tn), lambda j, i: (i, j)))
        args.append(res)
    in_specs.append(pl.BlockSpec((ms, k), lambda j, i: (0, 0)))
    args.append(xs)
    if res is not None:
        in_specs.append(pl.BlockSpec((ms, tn), lambda j, i: (0, j)))
        args.append(ress)
    return pl.pallas_call(
        functools.partial(_matmul_kernel, has_res=res is not None),
        out_shape=(jax.ShapeDtypeStruct((sections, m, ns), F32),
                   jax.ShapeDtypeStruct((sections, ms, ns), F32)),
        grid=(n // tn, m // tm),
        in_specs=in_specs,
        out_specs=(pl.BlockSpec((None, tm, tn), lambda j, i: (j // per, i, j % per)),
                   pl.BlockSpec((None, ms, tn), lambda j, i: (j // per, 0, j % per))),
        scratch_shapes=[pltpu.VMEM((k, tn), BF16)],
        compiler_params=_params("arbitrary", "arbitrary"),
        name="matmul",
    )(*args)


def _kv_heads_kernel(k_ref, v_ref, g_ref, *rest, heads):
    kb_ref, vb_ref, knh_ref, vh_ref = rest[-4:]
    g = g_ref[...]
    tm = k_ref.shape[0]
    for h in range(heads):
        cols = slice(h * HEAD_DIM, (h + 1) * HEAD_DIM)
        x = k_ref[:, cols]
        ms = jnp.mean(x * x, axis=-1, keepdims=True)
        kn = x * lax.rsqrt(ms + NORM_EPS) * g
        v = v_ref[:, cols]
        kb_ref[:, cols] = kn.astype(BF16)
        vb_ref[:, cols] = v.astype(BF16)
        knh_ref[pl.ds(h, tm, stride=heads), :] = kn
        vh_ref[pl.ds(h, tm, stride=heads), :] = v


def _kv_heads(qkv3, gains, layer, n_layers, stacks):
    _, m, d = qkv3.shape
    heads = d // HEAD_DIM
    tm = _pick(m, 256, SUBLANES)
    flat = jax.ShapeDtypeStruct((m, d), BF16)
    by_head = jax.ShapeDtypeStruct((n_layers, m * heads, HEAD_DIM), F32)
    flat_spec = pl.BlockSpec((tm, d), lambda i: (i, 0))
    head_spec = pl.BlockSpec((None, tm * heads, HEAD_DIM), lambda i: (layer, i, 0))
    in_specs = [pl.BlockSpec((None, tm, d), lambda i: (1, i, 0)),
                pl.BlockSpec((None, tm, d), lambda i: (2, i, 0)),
                pl.BlockSpec((None, 1, HEAD_DIM), lambda i: (layer, 0, 0))]
    args = [qkv3, qkv3, gains[:, None, :]]
    aliases = {}
    if stacks is not None:
        in_specs += [pl.BlockSpec(memory_space=pl.ANY)] * 2
        args += list(stacks)
        aliases = {3: 2, 4: 3}
    return pl.pallas_call(
        functools.partial(_kv_heads_kernel, heads=heads),
        out_shape=(flat, flat, by_head, by_head),
        grid=(m // tm,),
        in_specs=in_specs,
        out_specs=(flat_spec, flat_spec, head_spec, head_spec),
        input_output_aliases=aliases,
        compiler_params=_params("arbitrary"),
        name="kv_heads",
    )(*args)


SPLIT_PARTS = 2


def _split_lanes(x):
    return jnp.concatenate(_split_bf16(x, SPLIT_PARTS), axis=1)


def _suffix_ones(n):
    r = lax.broadcasted_iota(jnp.int32, (SPLIT_PARTS * n, n), 0) % n
    c = lax.broadcasted_iota(jnp.int32, (SPLIT_PARTS * n, n), 1)
    return jnp.where(r > c, 1.0, 0.0).astype(BF16)


def _nt_dot(a, b):
    return lax.dot_general(a, b, (((1,), (1,)), ((), ())), preferred_element_type=F32)


def _sb_log_keep(z, mask):
    sp = _softplus_log2(z)
    lk = -sp if mask is None else jnp.where(mask, -sp, 0.0)
    return lk, z - sp


def _sb_tail(lk, carry, u):
    return carry + jnp.dot(_split_lanes(lk), u, preferred_element_type=F32)


def _normed_query(q, gain):
    ms = jnp.mean(q * q, axis=-1, keepdims=True)
    return q * lax.rsqrt(ms + NORM_EPS) * gain * (HEAD_DIM ** -0.5 * LOG2_E)


def _sb_prompt_kernel(bias_ref, q_ref, k_ref, v_ref, qg_ref, o_ref, qb_ref, acc_ref, carry_ref,
                      *, group):
    hg = pl.program_id(1)
    i = pl.program_id(2)
    tile = q_ref.shape[0]
    u = _suffix_ones(tile)
    row = lax.broadcasted_iota(jnp.int32, (tile, tile), 0)
    col = lax.broadcasted_iota(jnp.int32, (tile, tile), 1)
    for g in range(group):
        cols = slice(g * HEAD_DIM, (g + 1) * HEAD_DIM)
        qb_ref[:, cols] = _normed_query(q_ref[:, cols], qg_ref[...]).astype(BF16)
    acc_ref[...] = jnp.zeros_like(acc_ref)
    carry_ref[...] = jnp.zeros_like(carry_ref)

    def key_tile(j, mask):
        keys = pl.ds(pl.multiple_of(j * tile, tile), tile)
        heads = [slice(g * HEAD_DIM, (g + 1) * HEAD_DIM) for g in range(group)]
        zs = [_nt_dot(qb_ref[:, cols], k_ref[keys, cols]) + bias_ref[hg * group + g] * LOG2_E
              for g, cols in enumerate(heads)]
        keeps = [_sb_log_keep(z, mask) for z in zs]
        tails = [_sb_tail(lk, carry_ref[g], u) for g, (lk, _) in enumerate(keeps)]
        for g, cols in enumerate(heads):
            lk, log_beta = keeps[g]
            a = jnp.exp2(log_beta + tails[g])
            if mask is not None:
                a = jnp.where(mask, a, 0.0)
            acc_ref[:, cols] += jnp.dot(a.astype(BF16), v_ref[keys, cols],
                                        preferred_element_type=F32)
            carry_ref[g] += jnp.sum(lk, axis=-1, keepdims=True)

    key_tile(i, col < row)

    def body(step, _):
        key_tile(i - 1 - step, None)
        return 0

    lax.fori_loop(0, i, body, 0)
    o_ref[...] = acc_ref[...].astype(o_ref.dtype)


def _sb_prompt(qkv3, kb, vb, q_gain, bias, layer, batch):
    _, m, d = qkv3.shape
    t = m // batch
    heads = d // HEAD_DIM
    tile = _pick(t, SB_TILE)
    nq = t // tile
    group = SB_HEADS_PER_STEP if heads % SB_HEADS_PER_STEP == 0 else 1
    width = group * HEAD_DIM
    kv_spec = pl.BlockSpec((t, width), lambda b, h, i: (b, h))
    return pl.pallas_call(
        functools.partial(_sb_prompt_kernel, group=group),
        out_shape=jax.ShapeDtypeStruct((m, d), BF16),
        grid=(batch, heads // group, nq),
        in_specs=[pl.BlockSpec(memory_space=pltpu.SMEM),
                  pl.BlockSpec((None, tile, width), lambda b, h, i: (0, b * nq + i, h)),
                  kv_spec,
                  kv_spec,
                  pl.BlockSpec((None, 1, HEAD_DIM), lambda b, h, i: (layer, 0, 0))],
        out_specs=pl.BlockSpec((tile, width), lambda b, h, i: (b * nq + i, h)),
        scratch_shapes=[pltpu.VMEM((tile, width), BF16),
                        pltpu.VMEM((tile, width), F32),
                        pltpu.VMEM((group, tile, 1), F32)],
        compiler_params=_params("arbitrary", "arbitrary", "arbitrary"),
        name="sb_prompt",
    )(bias[layer], qkv3, kb, vb, q_gain[:, None, :])


def _sb_sample_kernel(pt_ref, q_ref, kn_ref, vn_ref, qg_ref, bias_ref, hmask_ref, wsuf_ref, *rest,
                      heads, tq, page, pages_per_step):
    del pt_ref
    kc_refs = rest[:pages_per_step]
    vc_refs = rest[pages_per_step:2 * pages_per_step]
    o_ref, qall_ref, acc_ref, carry_ref, kpad_ref, vpad_ref = rest[2 * pages_per_step:]
    p = pl.program_id(1)
    lanes = page * heads
    n_blk = lanes // HEAD_DIM
    n_grp = heads * tq // SUBLANES

    def tiles(kv_refs, valid):
        logits = []
        for k_ref, _ in kv_refs:
            f = _nt_dot(qall_ref[...].astype(BF16), k_ref[...].astype(BF16)) * hmask_ref[...]
            zs = f[0:SUBLANES]
            for g in range(1, n_grp):
                zs = zs + f[g * SUBLANES:(g + 1) * SUBLANES]
            z = zs
            for k in range(1, SUBLANES // tq):
                z = z + pltpu.roll(zs, k * tq, axis=0)
            z = z + jnp.concatenate([bias_ref[...] * LOG2_E] * n_blk, axis=1)
            logits.append(_sb_log_keep(z, valid))
        sums = []
        for lk, _ in logits:
            blocks = jnp.concatenate(
                [lk[:, b * HEAD_DIM:(b + 1) * HEAD_DIM] for b in range(n_blk)], axis=0)
            sums.append(jnp.dot(_split_lanes(blocks), wsuf_ref[...], preferred_element_type=F32))
        run = carry_ref[...]
        out = jnp.zeros(acc_ref.shape, F32)
        for (_, log_beta), res, (_, v_ref) in zip(logits, sums, kv_refs):
            tails = [None] * n_blk
            for b in reversed(range(n_blk)):
                blk = res[b * SUBLANES:(b + 1) * SUBLANES]
                tails[b] = blk[:, :HEAD_DIM] + run
                run = run + blk[:, HEAD_DIM:]
            a = jnp.exp2(log_beta + jnp.concatenate(tails, axis=1))
            if valid is not None:
                a = jnp.where(valid, a, 0.0)
            a_heads = (jnp.concatenate([a] * n_grp, axis=0) * hmask_ref[...]).astype(BF16)
            out = out + jnp.dot(a_heads, v_ref[...].astype(BF16), preferred_element_type=F32)
        carry_ref[...] = run
        acc_ref[...] += out

    @pl.when(p == 0)
    def _new_tokens():
        for h in range(heads):
            cols = slice(h * HEAD_DIM, (h + 1) * HEAD_DIM)
            qall_ref[h * tq:(h + 1) * tq, :] = _normed_query(q_ref[:, cols], qg_ref[...])
        kpad_ref[...] = jnp.zeros_like(kpad_ref)
        vpad_ref[...] = jnp.zeros_like(vpad_ref)
        kpad_ref[0:tq * heads, :] = kn_ref[...]
        vpad_ref[0:tq * heads, :] = vn_ref[...]
        acc_ref[...] = jnp.zeros_like(acc_ref)
        carry_ref[...] = jnp.zeros_like(carry_ref)
        query = lax.broadcasted_iota(jnp.int32, (SUBLANES, lanes), 0) % tq
        position = lax.broadcasted_iota(jnp.int32, (SUBLANES, lanes), 1) // heads
        tiles([(kpad_ref, vpad_ref)], position < query)

    @pl.when(p > 0)
    def _past_page():
        tiles(list(zip(kc_refs, vc_refs)), None)

    @pl.when(p == pl.num_programs(1) - 1)
    def _finish():
        for h in range(heads):
            o_ref[:, h * HEAD_DIM:(h + 1) * HEAD_DIM] = acc_ref[h * tq:(h + 1) * tq, :]


def _sb_sample(qkv3, kn, vh, q_gain, bias, cache_k, cache_v, page_table, layer, batch):
    _, m, d = qkv3.shape
    tq = m // batch
    heads = d // HEAD_DIM
    assert SUBLANES % tq == 0 and (heads * tq) % SUBLANES == 0 and HEAD_DIM % heads == 0
    rows = heads * tq
    n_pages = page_table.shape[1]
    lanes = cache_k.shape[2]
    page = lanes // heads
    qkv4 = qkv3.reshape(3, batch, tq, d)
    lane_head = np.arange(lanes) % heads
    bias_tile = jnp.broadcast_to(jnp.tile(bias[layer], HEAD_DIM // heads)[None, :],
                                 (SUBLANES, HEAD_DIM)).astype(F32)
    hmask = jnp.asarray(lane_head[None, :] == (np.arange(rows) // tq)[:, None], F32)
    l = np.arange(HEAD_DIM)
    same_head = (l[:, None] % heads) == (l[None, :] % heads)
    later = (l[:, None] // heads) > (l[None, :] // heads)
    wsuf = np.concatenate([same_head & later, same_head], axis=1)
    wsuf = jnp.asarray(np.concatenate([wsuf] * SPLIT_PARTS, axis=0), BF16)

    pps = SB_PAGES_PER_STEP if n_pages % SB_PAGES_PER_STEP == 0 else 1

    def page_index(j):
        return lambda b, p, pt: (layer, pt[b, n_pages - 1 - ((jnp.maximum(p, 1) - 1) * pps + j)], 0, 0)

    page_specs = [pl.BlockSpec((None, None, lanes, HEAD_DIM), page_index(j)) for j in range(pps)]
    fixed = lambda shape: pl.BlockSpec(shape, lambda b, p, pt: (0,) * len(shape))
    new_spec = pl.BlockSpec((None, None, tq * heads, HEAD_DIM), lambda b, p, pt: (layer, b, 0, 0))
    stack_shape = (kn.shape[0], batch, tq * heads, HEAD_DIM)
    grid_spec = pltpu.PrefetchScalarGridSpec(
        num_scalar_prefetch=1,
        grid=(batch, n_pages // pps + 1),
        in_specs=[pl.BlockSpec((None, None, tq, d), lambda b, p, pt: (0, b, 0, 0)),
                  new_spec,
                  new_spec,
                  pl.BlockSpec((None, 1, HEAD_DIM), lambda b, p, pt: (layer, 0, 0)),
                  fixed(bias_tile.shape),
                  fixed(hmask.shape),
                  fixed(wsuf.shape)] + page_specs + page_specs,
        out_specs=pl.BlockSpec((None, tq, d), lambda b, p, pt: (b, 0, 0)),
        scratch_shapes=[pltpu.VMEM((rows, HEAD_DIM), F32),
                        pltpu.VMEM((rows, HEAD_DIM), F32),
                        pltpu.VMEM((SUBLANES, HEAD_DIM), F32),
                        pltpu.VMEM((lanes, HEAD_DIM), F32),
                        pltpu.VMEM((lanes, HEAD_DIM), F32)])
    out = pl.pallas_call(
        functools.partial(_sb_sample_kernel, heads=heads, tq=tq, page=page, pages_per_step=pps),
        out_shape=jax.ShapeDtypeStruct((batch, tq, d), F32),
        grid_spec=grid_spec,
        compiler_params=_params("arbitrary", "arbitrary"),
        name="sb_sample",
    )(page_table, qkv4, kn.reshape(stack_shape), vh.reshape(stack_shape),
      q_gain[:, None, :], bias_tile, hmask, wsuf, *([cache_k] * pps), *([cache_v] * pps))
    return out.reshape(m, d)


def _lower_bounds_kernel(x_ref, lb_ref):
    x = x_ref[...]
    e = jnp.exp(x - jnp.max(x, axis=0, keepdims=True))
    soft = e / jnp.sum(e, axis=0, keepdims=True)
    depth = x.shape[0]
    run = jnp.zeros_like(soft[0:1])
    for layer in range(depth):
        run = run + soft[layer:layer + 1]
        lb = run - soft[0:1]
        lb_ref[layer, 0:1, :] = lb
        lb_ref[layer, 1:2, :] = jnp.log(lb)
        lb_ref[layer, 2:3, :] = jnp.log1p(-lb)
        lb_ref[layer, 3:4, :] = 1.0 - lb
        lb_ref[layer, 4:8, :] = jnp.zeros((4, x.shape[1]), F32)


def _lower_bounds(hg_lower_bounds):
    depth, d = hg_lower_bounds.shape
    return pl.pallas_call(
        _lower_bounds_kernel,
        out_shape=jax.ShapeDtypeStruct((depth, SUBLANES, d), F32),
        name="hg_lower_bounds",
    )(hg_lower_bounds.astype(F32))


def _hg_gates(f, lbp):
    log_lb, log_1m, one_m = lbp[1:2], lbp[2:3], lbp[3:4]
    log_sig = jnp.minimum(f, 0.0) - jnp.log(1.0 + jnp.exp(-jnp.abs(f)))
    b = log_1m + log_sig
    g = jnp.maximum(log_lb, b) + jnp.log(1.0 + jnp.exp(-jnp.abs(log_lb - b)))
    key = one_m * (1.0 / (1.0 + jnp.exp(f)))
    return g, key


def _hg_output(o, gate, gain):
    ms = jnp.mean(o * o, axis=-1, keepdims=True)
    on = o * lax.rsqrt(ms + NORM_EPS) * gain
    return on * (gate * (1.0 / (1.0 + jnp.exp(-gate))))


def _hg_prompt_kernel(q_ref, f_ref, i_ref, g_ref, lbp_ref, gain_ref, o_ref, s_ref,
                      st_ref, gc_ref, *, chunk, sub, group):
    t = q_ref.shape[0]
    n_sub = chunk // sub
    n_chunks = t // chunk
    gain = gain_ref[...]
    r = lax.broadcasted_iota(jnp.int32, (chunk, chunk), 0)
    c = lax.broadcasted_iota(jnp.int32, (chunk, chunk), 1)
    lower = jnp.where(c <= r, 1.0, 0.0).astype(BF16)
    crow = lax.broadcasted_iota(jnp.int32, (chunk, HEAD_DIM), 0)
    srow = lax.broadcasted_iota(jnp.int32, (sub, HEAD_DIM), 0)
    own = (lax.broadcasted_iota(jnp.int32, (sub, sub * sub), 1) // sub
           == lax.broadcasted_iota(jnp.int32, (sub, sub * sub), 0))
    st_ref[...] = jnp.zeros_like(st_ref)

    def head_cols(hi):
        return slice(hi * HEAD_DIM, (hi + 1) * HEAD_DIM)

    def decays(ci):
        rows = pl.ds(pl.multiple_of(ci * chunk, chunk), chunk)
        gates = [_hg_gates(f_ref[rows, head_cols(hi)], lbp_ref[:, head_cols(hi)])
                 for hi in range(group)]
        parts = [p for glog, _ in gates for p in _split_bf16(glog * LOG2_E, 3)]
        sums = jnp.dot(lower, jnp.concatenate(parts, axis=1), preferred_element_type=F32)
        out = []
        for hi, (_, kk) in enumerate(gates):
            gc = sums[:, head_cols(3 * hi)] + sums[:, head_cols(3 * hi + 1)] + sums[:, head_cols(3 * hi + 2)]
            out.append((gc, kk))
        return tuple(out)

    def scores(r0, hi, gc, kk):
        rows = pl.ds(r0, chunk)
        gc_ref[hi] = gc
        q = q_ref[rows, head_cols(hi)]
        vb = i_ref[rows, head_cols(hi)].astype(BF16)
        st = st_ref[hi]
        inter = _nt_dot((q * jnp.exp2(gc)).astype(BF16), st.astype(BF16))
        g_last = gc[chunk - 1:chunk, :]
        offs, diags = [], []
        for si in range(n_sub):
            base = si * sub
            blk = slice(base, base + sub)
            q_s, g_s, k_s = q[blk], gc[blk], kk[blk]
            if si > 0:
                g_start = gc[base - 1:base, :]
                k_prev = jnp.where(crow < base, kk * jnp.exp2(g_start - gc), 0.0)
                q_in = q_s * jnp.exp2(g_s - g_start)
                offs.append(_nt_dot(q_in.astype(BF16), k_prev.astype(BF16)))
            else:
                offs.append(None)
            stacked = []
            for ti in range(sub):
                g_t = gc_ref[hi, base + ti:base + ti + 1, :]
                k_t = jnp.where(srow <= ti, k_s * jnp.exp2(g_t - g_s), 0.0)
                stacked.append(k_t.astype(BF16))
            diags.append(_nt_dot(q_s.astype(BF16), jnp.concatenate(stacked, axis=0)))
        k_dec = kk * jnp.exp2(g_last - gc)
        st_ref[hi] = st * jnp.exp2(g_last) + lax.dot_general(
            vb, k_dec.astype(BF16), (((0,), (0,)), ((), ())), preferred_element_type=F32)
        return vb, inter, offs, diags

    def outputs(r0, hi, vb, inter, offs, diags):
        for si in range(n_sub):
            base = si * sub
            blk = slice(base, base + sub)
            o_s = inter[blk]
            if offs[si] is not None:
                o_s = o_s + jnp.dot(offs[si].astype(BF16), vb, preferred_element_type=F32)
            sc = jnp.where(own, diags[si], 0.0).astype(BF16)
            o_s = o_s + jnp.dot(sc, jnp.concatenate([vb[blk]] * sub, axis=0),
                                preferred_element_type=F32)
            out_rows = pl.ds(pl.multiple_of(r0 + base, sub), sub)
            o_ref[out_rows, head_cols(hi)] = _hg_output(
                o_s, g_ref[out_rows, head_cols(hi)], gain).astype(o_ref.dtype)

    def body(ci, current):
        r0 = pl.multiple_of(ci * chunk, chunk)
        following = decays(jnp.minimum(ci + 1, n_chunks - 1))
        staged = [scores(r0, hi, *current[hi]) for hi in range(group)]
        for hi in range(group):
            outputs(r0, hi, *staged[hi])
        return following

    lax.fori_loop(0, n_chunks, body, decays(0))
    for hi in range(group):
        s_ref[hi] = st_ref[hi].T


def _hg_prompt(proj, lbp, out_gain, layer, hg_index, batch):
    m, d4 = proj.shape
    d = d4 // 4
    heads = d // HEAD_DIM
    t = m // batch
    chunk = _pick(t, HG_CHUNK, HG_SUB)
    sub = HG_SUB
    group = HG_HEADS_PER_STEP if heads % HG_HEADS_PER_STEP == 0 else 1
    width = group * HEAD_DIM
    ng = heads // group
    col = lambda sec: (lambda b, h: (b, sec * ng + h))
    return pl.pallas_call(
        functools.partial(_hg_prompt_kernel, chunk=chunk, sub=sub, group=group),
        out_shape=(jax.ShapeDtypeStruct((m, d), BF16),
                   jax.ShapeDtypeStruct((batch, heads, HEAD_DIM, HEAD_DIM), F32)),
        grid=(batch, ng),
        in_specs=[pl.BlockSpec((t, width), col(0)),
                  pl.BlockSpec((t, width), col(1)),
                  pl.BlockSpec((t, width), col(2)),
                  pl.BlockSpec((t, width), col(3)),
                  pl.BlockSpec((None, SUBLANES, width), lambda b, h: (layer, 0, h)),
                  pl.BlockSpec((None, 1, HEAD_DIM), lambda b, h: (hg_index, 0, 0))],
        out_specs=(pl.BlockSpec((t, width), lambda b, h: (b, h)),
                   pl.BlockSpec((None, group, HEAD_DIM, HEAD_DIM), lambda b, h: (b, h, 0, 0))),
        scratch_shapes=[pltpu.VMEM((group, HEAD_DIM, HEAD_DIM), F32),
                        pltpu.VMEM((group, chunk, HEAD_DIM), F32)],
        compiler_params=_params("arbitrary", "arbitrary"),
        name="hg_prompt",
    )(proj, proj, proj, proj, lbp, out_gain[:, None, :])


def _hg_sample_kernel(q_ref, f_ref, i_ref, g_ref, lbp_ref, gain_ref, s0_ref, o_ref, s_ref, pad_ref,
                      *, heads):
    tq = q_ref.shape[0]
    pad_ref[...] = jnp.zeros_like(pad_ref)

    def columns(slot, x):
        pad_ref[slot, 0:tq, :] = x
        return pad_ref[slot].T

    for h in range(heads):
        cols = slice(h * HEAD_DIM, (h + 1) * HEAD_DIM)
        glog, kk = _hg_gates(f_ref[:, cols], lbp_ref[:, cols])
        q_c = columns(3 * h, q_ref[:, cols])
        f_c = columns(3 * h + 1, jnp.exp(glog))
        k_c = columns(3 * h + 2, kk)
        v = i_ref[:, cols]
        s = s0_ref[h]
        for ti in range(tq):
            s = f_c[:, ti:ti + 1] * s + k_c[:, ti:ti + 1] * v[ti:ti + 1, :]
            o = jnp.sum(q_c[:, ti:ti + 1] * s, axis=0, keepdims=True)
            o_ref[ti:ti + 1, cols] = _hg_output(o, g_ref[ti:ti + 1, cols], gain_ref[...])
        s_ref[h] = s


def _hg_sample(proj, lbp, out_gain, state, layer, hg_index, batch):
    m, d4 = proj.shape
    d = d4 // 4
    heads = d // HEAD_DIM
    tq = m // batch
    proj3 = proj.reshape(batch, tq, d4)
    col = lambda sec: (lambda b: (b, 0, sec))
    state_block = (None, heads, HEAD_DIM, HEAD_DIM)
    out, s_new = pl.pallas_call(
        functools.partial(_hg_sample_kernel, heads=heads),
        out_shape=(jax.ShapeDtypeStruct((batch, tq, d), F32),
                   jax.ShapeDtypeStruct((batch, heads, HEAD_DIM, HEAD_DIM), F32)),
        grid=(batch,),
        in_specs=[pl.BlockSpec((None, tq, d), col(0)),
                  pl.BlockSpec((None, tq, d), col(1)),
                  pl.BlockSpec((None, tq, d), col(2)),
                  pl.BlockSpec((None, tq, d), col(3)),
                  pl.BlockSpec((None, SUBLANES, d), lambda b: (layer, 0, 0)),
                  pl.BlockSpec((None, 1, HEAD_DIM), lambda b: (hg_index, 0, 0)),
                  pl.BlockSpec((None,) + state_block, lambda b: (hg_index, b, 0, 0, 0))],
        out_specs=(pl.BlockSpec((None, tq, d), lambda b: (b, 0, 0)),
                   pl.BlockSpec(state_block, lambda b: (b, 0, 0, 0))),
        scratch_shapes=[pltpu.VMEM((3 * heads, HEAD_DIM, HEAD_DIM), F32)],
        compiler_params=_params("arbitrary"),
        name="hg_sample",
    )(proj3, proj3, proj3, proj3, lbp, out_gain[:, None, :], state)
    return out.reshape(m, d), s_new


def _ffn_gate_kernel(a_ref, b_ref, buf_ref, w_ref, cb_ref, o_ref, st_ref, ext_ref):
    t = a_ref.shape[0]
    ext_ref[CONV_LEAD:SUBLANES, :] = buf_ref[...]
    o_ref[...] = _conv_gate(ext_ref, a_ref[...], b_ref[...], w_ref, cb_ref).astype(o_ref.dtype)
    st_ref[...] = ext_ref[SUBLANES + t - (FFN_CONV_W - 1):SUBLANES + t, :]


CONV_LEAD = SUBLANES - (FFN_CONV_W - 1)


def _conv_gate(ext_ref, a, b, w_ref, cb_ref):
    t = a.shape[0]
    lead = CONV_LEAD
    ext_ref[SUBLANES:SUBLANES + t, :] = a
    c = cb_ref[...]
    for j in range(FFN_CONV_W - 1):
        c = c + w_ref[j:j + 1, :] * ext_ref[lead + j:lead + j + t, :]
    c = c + w_ref[FFN_CONV_W - 1:FFN_CONV_W, :] * a
    return c * (1.0 / (1.0 + jnp.exp(-c))) * b


def _ffn_in_kernel(x_ref, wa_ref, wb_ref, xs_ref, cw_ref, cb_ref, o_ref, st_ref, as_ref, bs_ref,
                   wa16_ref, wb16_ref, hist_ref, *, tiles_per_seq, row_blocks):
    i = pl.program_id(1)
    tm = x_ref.shape[0]

    @pl.when(i == 0)
    def _():
        wa16_ref[...] = wa_ref[...].astype(BF16)
        wb16_ref[...] = wb_ref[...].astype(BF16)
        xs = xs_ref[...].astype(BF16)
        as_ref[...] = jnp.dot(xs, wa16_ref[...], preferred_element_type=F32)
        bs_ref[...] = jnp.dot(xs, wb16_ref[...], preferred_element_type=F32)

    @pl.when(i % tiles_per_seq == 0)
    def _():
        hist_ref[...] = jnp.zeros_like(hist_ref)

    rb = tm // row_blocks
    row = lax.broadcasted_iota(jnp.int32, (SUBLANES, o_ref.shape[1]), 0)

    def products(r):
        x = x_ref[r * rb:(r + 1) * rb, :]
        return (jnp.dot(x, wa16_ref[...], preferred_element_type=F32),
                jnp.dot(x, wb16_ref[...], preferred_element_type=F32))

    def shifted(a, first_rows):
        k = len(first_rows)
        moved = pltpu.roll(a, k, axis=0)
        top = moved[0:SUBLANES]
        for j, r in enumerate(first_rows):
            top = jnp.where(row == j, r, top)
        return jnp.concatenate([top, moved[SUBLANES:]], axis=0)

    def gate(r, a, b, before):
        c = cb_ref[...]
        c = c + cw_ref[0:1, :] * shifted(a, [before[0:1], before[1:2]])
        c = c + cw_ref[1:2, :] * shifted(a, [before[1:2]])
        c = c + cw_ref[2:3, :] * a
        o_ref[r * rb:(r + 1) * rb, :] = (c * (1.0 / (1.0 + jnp.exp(-c))) * b).astype(o_ref.dtype)

    before = hist_ref[...]
    pending = products(0)
    for r in range(1, row_blocks):
        issued = products(r)
        gate(r - 1, *pending, before)
        before = pending[0][rb - (FFN_CONV_W - 1):rb, :]
        pending = issued
    gate(row_blocks - 1, *pending, before)
    last = pending[0][rb - (FFN_CONV_W - 1):rb, :]
    hist_ref[...] = last
    st_ref[...] = last


def _ffn_in(x, xs, w_in, conv_w, conv_b, layer, batch):
    m, k = x.shape
    ms = xs.shape[0]
    f = w_in.shape[-1] // 2
    t = m // batch
    tm = _pick(t, 1024, SUBLANES)
    tps = t // tm
    tf = _pick(f, 512)
    nf = f // tf
    return pl.pallas_call(
        functools.partial(_ffn_in_kernel, tiles_per_seq=tps,
                          row_blocks=FFN_ROW_BLOCKS if tm % (FFN_ROW_BLOCKS * SUBLANES) == 0 else 1),
        out_shape=(jax.ShapeDtypeStruct((m, f), BF16),
                   jax.ShapeDtypeStruct((batch, FFN_CONV_W - 1, f), F32),
                   jax.ShapeDtypeStruct((ms, f), F32),
                   jax.ShapeDtypeStruct((ms, f), F32)),
        grid=(nf, m // tm),
        in_specs=[pl.BlockSpec((tm, k), lambda j, i: (i, 0)),
                  pl.BlockSpec((None, k, tf), lambda j, i: (layer, 0, j)),
                  pl.BlockSpec((None, k, tf), lambda j, i: (layer, 0, nf + j)),
                  pl.BlockSpec((ms, k), lambda j, i: (0, 0)),
                  pl.BlockSpec((None, FFN_CONV_W, tf), lambda j, i: (layer, 0, j)),
                  pl.BlockSpec((None, 1, tf), lambda j, i: (layer, 0, j))],
        out_specs=(pl.BlockSpec((tm, tf), lambda j, i: (i, j)),
                   pl.BlockSpec((None, FFN_CONV_W - 1, tf), lambda j, i: (i // tps, 0, j)),
                   pl.BlockSpec((ms, tf), lambda j, i: (0, j)),
                   pl.BlockSpec((ms, tf), lambda j, i: (0, j))),
        scratch_shapes=[pltpu.VMEM((k, tf), BF16),
                        pltpu.VMEM((k, tf), BF16),
                        pltpu.VMEM((FFN_CONV_W - 1, tf), F32)],
        compiler_params=_params("arbitrary", "arbitrary"),
        name="ffn_in",
    )(x, w_in, w_in, xs, conv_w, conv_b[:, None, :])


def _ffn_gate(a, b, buf, conv_w, conv_b, layer, batch):
    m, f = a.shape
    t = m // batch
    row_spec = pl.BlockSpec((None, t, f), lambda i: (i, 0, 0))
    hist_spec = pl.BlockSpec((None, FFN_CONV_W - 1, f), lambda i: (i, 0, 0))
    out, st = pl.pallas_call(
        _ffn_gate_kernel,
        out_shape=(jax.ShapeDtypeStruct((batch, t, f), F32),
                   jax.ShapeDtypeStruct((batch, FFN_CONV_W - 1, f), F32)),
        grid=(batch,),
        in_specs=[row_spec,
                  row_spec,
                  pl.BlockSpec((None, None, FFN_CONV_W - 1, f), lambda i: (layer, i, 0, 0)),
                  pl.BlockSpec((None, FFN_CONV_W, f), lambda i: (layer, 0, 0)),
                  pl.BlockSpec((None, 1, f), lambda i: (layer, 0, 0))],
        out_specs=(row_spec, hist_spec),
        scratch_shapes=[pltpu.VMEM((t + SUBLANES, f), F32)],
        compiler_params=_params("arbitrary"),
        name="ffn_gate",
    )(a.reshape(batch, t, f), b.reshape(batch, t, f), buf, conv_w, conv_b[:, None, :])
    return out.reshape(m, f), st


def kernel(x_prompt, x_sample, cache_sb_k, cache_sb_v, page_table, state_hgrn, state_ffn_conv,
           norm_mixer, norm_ffn, w_sb_qkv, sb_q_gain, sb_k_gain, sb_logit_bias, w_sb_o,
           w_hg_in, hg_lower_bounds, hg_out_gain, w_hg_o,
           w_ffn_in, ffn_conv_w, ffn_conv_b, w_ffn_out):
    bp, tp, d = x_prompt.shape
    bs, ts, _ = x_sample.shape
    depth = norm_mixer.shape[0]
    heads = d // HEAD_DIM
    n_mixers = 2
    n_sb, pool, page = cache_sb_k.shape[:3]
    cache_k = cache_sb_k.reshape(n_sb, pool, page * heads, HEAD_DIM)
    cache_v = cache_sb_v.reshape(n_sb, pool, page * heads, HEAD_DIM)
    lbp = _lower_bounds(hg_lower_bounds)

    xp = x_prompt.reshape(bp * tp, d)
    xs = x_sample.reshape(bs * ts, d)
    kv_p = kv_s = None
    sp_l, ss_l, cp_l, cs_l = [], [], [], []
    for layer in range(depth):
        j = layer // n_mixers
        hp = _rmsnorm(xp, norm_mixer, layer)
        hs = _rmsnorm(xs, norm_mixer, layer)
        if layer % n_mixers == 0:
            qkv_p, qkv_s = _matmul(hp, hs, w_sb_qkv, j, sections=3)
            kb_p, vb_p, *kv_p = _kv_heads(qkv_p, sb_k_gain, j, n_sb, kv_p)
            _, _, *kv_s = _kv_heads(qkv_s, sb_k_gain, j, n_sb, kv_s)
            op = _sb_prompt(qkv_p, kb_p, vb_p, sb_q_gain, sb_logit_bias, j, bp)
            os_ = _sb_sample(qkv_s, kv_s[0], kv_s[1], sb_q_gain, sb_logit_bias, cache_k, cache_v,
                             page_table, j, bs)
            xp, xs = _matmul(op, os_, w_sb_o, j, res=xp, ress=xs)
        else:
            proj_p, proj_s = _matmul(hp, hs, w_hg_in, j)
            op, sp = _hg_prompt(proj_p[0], lbp, hg_out_gain, layer, j, bp)
            os_, ss = _hg_sample(proj_s[0], lbp, hg_out_gain, state_hgrn, layer, j, bs)
            xp, xs = _matmul(op, os_, w_hg_o, j, res=xp, ress=xs)
            sp_l.append(sp)
            ss_l.append(ss)
        xp, xs = xp[0], xs[0]
        hp = _rmsnorm(xp, norm_ffn, layer)
        hs = _rmsnorm(xs, norm_ffn, layer)
        gp, cp, a_s, b_s = _ffn_in(hp, hs, w_ffn_in, ffn_conv_w, ffn_conv_b, layer, bp)
        gs, cs = _ffn_gate(a_s, b_s, state_ffn_conv, ffn_conv_w, ffn_conv_b, layer, bs)
        xp, xs = _matmul(gp, gs, w_ffn_out, layer, res=xp, ress=xs, tm_target=512, tn_target=512)
        xp, xs = xp[0], xs[0]
        cp_l.append(cp)
        cs_l.append(cs)

    kv_shape_p = (n_sb, bp, tp, heads, HEAD_DIM)
    kv_shape_s = (n_sb, bs, ts, heads, HEAD_DIM)
    return (xp.reshape(bp, tp, d), xs.reshape(bs, ts, d),
            kv_p[0].reshape(kv_shape_p), kv_p[1].reshape(kv_shape_p),
            kv_s[0].reshape(kv_shape_s), kv_s[1].reshape(kv_shape_s),
            jnp.stack(sp_l), jnp.stack(ss_l), jnp.stack(cp_l), jnp.stack(cs_l))
```

```python
import functools

import jax
import jax.numpy as jnp
import numpy as np
from jax import lax
from jax.experimental import pallas as pl
from jax.experimental.pallas import tpu as pltpu

F32 = jnp.float32
BF16 = jnp.bfloat16

HEAD_DIM = 128
SUBLANES = 8
NORM_EPS = 1e-6
FFN_CONV_W = 3
SB_TILE = 256
SB_HEADS_PER_STEP = 4
SB_PAGES_PER_STEP = 4
FFN_ROW_BLOCKS = 8
HG_HEADS_PER_STEP = 4
HG_CHUNK = 64
HG_SUB = 16
VMEM_LIMIT_BYTES = 56 * 1024 * 1024


def _params(*semantics):
    return pltpu.CompilerParams(dimension_semantics=semantics, vmem_limit_bytes=VMEM_LIMIT_BYTES)


def _pick(n, target, quantum=HEAD_DIM):
    best = None
    for d in range(quantum, min(n, target) + 1, quantum):
        if n % d == 0:
            best = d
    return best if best is not None else n


LOG2_E = 1.4426950408889634


def _softplus_log2(z):
    sign_bit = jnp.int32(-2 ** 31)
    neg_abs = lax.bitcast_convert_type(lax.bitcast_convert_type(z, jnp.int32) | sign_bit, F32)
    return jnp.maximum(z, 0.0) + jnp.log2(1.0 + jnp.exp2(neg_abs))


def _split_bf16(x, parts):
    out = []
    r = x
    for _ in range(parts - 1):
        p = r.astype(BF16)
        out.append(p)
        r = r - p.astype(F32)
    out.append(r.astype(BF16))
    return out


def _rmsnorm_kernel(x_ref, g_ref, o_ref):
    x = x_ref[...]
    ms = jnp.mean(x * x, axis=-1, keepdims=True)
    o_ref[...] = (x * lax.rsqrt(ms + NORM_EPS) * g_ref[...]).astype(o_ref.dtype)


def _rmsnorm(x, gains, layer):
    m, d = x.shape
    tm = _pick(m, 512, SUBLANES)
    return pl.pallas_call(
        _rmsnorm_kernel,
        out_shape=jax.ShapeDtypeStruct((m, d), BF16),
        grid=(m // tm,),
        in_specs=[pl.BlockSpec((tm, d), lambda i: (i, 0)),
                  pl.BlockSpec((None, 1, d), lambda i: (layer, 0, 0))],
        out_specs=pl.BlockSpec((tm, d), lambda i: (i, 0)),
        compiler_params=_params("arbitrary"),
        name="rmsnorm",
    )(x, gains[:, None, :])


def _matmul_kernel(*refs, has_res):
    if has_res:
        x_ref, w_ref, r_ref, xs_ref, rs_ref, o_ref, os_ref, wb_ref = refs
    else:
        x_ref, w_ref, xs_ref, o_ref, os_ref, wb_ref = refs

    @pl.when(pl.program_id(1) == 0)
    def _():
        wb_ref[...] = w_ref[...].astype(BF16)
        acc_s = jnp.dot(xs_ref[...].astype(BF16), wb_ref[...], preferred_element_type=F32)
        if has_res:
            acc_s = acc_s + rs_ref[...]
        os_ref[...] = acc_s

    acc = jnp.dot(x_ref[...].astype(BF16), wb_ref[...], preferred_element_type=F32)
    if has_res:
        acc = acc + r_ref[...]
    o_ref[...] = acc.astype(o_ref.dtype)


def _matmul(x, xs, w, layer, *, res=None, ress=None, sections=1, tm_target=1024, tn_target=1024):
    m, k = x.shape
    ms = xs.shape[0]
    n = w.shape[-1]
    ns = n // sections
    tm = _pick(m, tm_target, SUBLANES)
    tn = _pick(ns, tn_target)
    per = ns // tn
    in_specs = [pl.BlockSpec((tm, k), lambda j, i: (i, 0)),
                pl.BlockSpec((None, k, tn), lambda j, i: (layer, 0, j))]
    args = [x, w]
    if res is not None:
        assert sections == 1
        in_specs.append(pl.BlockSpec((tm, tn), lambda j, i: (i, j)))
        args.append(res)
    in_specs.append(pl.BlockSpec((ms, k), lambda j, i: (0, 0)))
    args.append(xs)
    if res is not None:
        in_specs.append(pl.BlockSpec((ms, tn), lambda j, i: (0, j)))
        args.append(ress)
    return pl.pallas_call(
        functools.partial(_matmul_kernel, has_res=res is not None),
        out_shape=(jax.ShapeDtypeStruct((sections, m, ns), F32),
                   jax.ShapeDtypeStruct((sections, ms, ns), F32)),
        grid=(n // tn, m // tm),
        in_specs=in_specs,
        out_specs=(pl.BlockSpec((None, tm, tn), lambda j, i: (j // per, i, j % per)),
                   pl.BlockSpec((None, ms, tn), lambda j, i: (j // per, 0, j % per))),
        scratch_shapes=[pltpu.VMEM((k, tn), BF16)],
        compiler_params=_params("arbitrary", "arbitrary"),
        name="matmul",
    )(*args)


def _kv_heads_kernel(k_ref, v_ref, g_ref, *rest, heads):
    kb_ref, vb_ref, knh_ref, vh_ref = rest[-4:]
    g = g_ref[...]
    tm = k_ref.shape[0]
    for h in range(heads):
        cols = slice(h * HEAD_DIM, (h + 1) * HEAD_DIM)
        x = k_ref[:, cols]
        ms = jnp.mean(x * x, axis=-1, keepdims=True)
        kn = x * lax.rsqrt(ms + NORM_EPS) * g
        v = v_ref[:, cols]
        kb_ref[:, cols] = kn.astype(BF16)
        vb_ref[:, cols] = v.astype(BF16)
        knh_ref[pl.ds(h, tm, stride=heads), :] = kn
        vh_ref[pl.ds(h, tm, stride=heads), :] = v


def _kv_heads(qkv3, gains, layer, n_layers, stacks):
    _, m, d = qkv3.shape
    heads = d // HEAD_DIM
    tm = _pick(m, 256, SUBLANES)
    flat = jax.ShapeDtypeStruct((m, d), BF16)
    by_head = jax.ShapeDtypeStruct((n_layers, m * heads, HEAD_DIM), F32)
    flat_spec = pl.BlockSpec((tm, d), lambda i: (i, 0))
    head_spec = pl.BlockSpec((None, tm * heads, HEAD_DIM), lambda i: (layer, i, 0))
    in_specs = [pl.BlockSpec((None, tm, d), lambda i: (1, i, 0)),
                pl.BlockSpec((None, tm, d), lambda i: (2, i, 0)),
                pl.BlockSpec((None, 1, HEAD_DIM), lambda i: (layer, 0, 0))]
    args = [qkv3, qkv3, gains[:, None, :]]
    aliases = {}
    if stacks is not None:
        in_specs += [pl.BlockSpec(memory_space=pl.ANY)] * 2
        args += list(stacks)
        aliases = {3: 2, 4: 3}
    return pl.pallas_call(
        functools.partial(_kv_heads_kernel, heads=heads),
        out_shape=(flat, flat, by_head, by_head),
        grid=(m // tm,),
        in_specs=in_specs,
        out_specs=(flat_spec, flat_spec, head_spec, head_spec),
        input_output_aliases=aliases,
        compiler_params=_params("arbitrary"),
        name="kv_heads",
    )(*args)


SPLIT_PARTS = 2


def _split_lanes(x):
    return jnp.concatenate(_split_bf16(x, SPLIT_PARTS), axis=1)


def _suffix_ones(n):
    r = lax.broadcasted_iota(jnp.int32, (SPLIT_PARTS * n, n), 0) % n
    c = lax.broadcasted_iota(jnp.int32, (SPLIT_PARTS * n, n), 1)
    return jnp.where(r > c, 1.0, 0.0).astype(BF16)


def _nt_dot(a, b):
    return lax.dot_general(a, b, (((1,), (1,)), ((), ())), preferred_element_type=F32)


def _sb_log_keep(z, mask):
    sp = _softplus_log2(z)
    lk = sp if mask is None else jnp.where(mask, sp, 0.0)
    return lk, z - sp


def _sb_tail(lk, carry, u):
    return carry + jnp.dot(_split_lanes(lk), u, preferred_element_type=F32)


def _normed_query(q, gain):
    ms = jnp.mean(q * q, axis=-1, keepdims=True)
    return q * lax.rsqrt(ms + NORM_EPS) * gain * (HEAD_DIM ** -0.5 * LOG2_E)


def _sb_prompt_kernel(bias_ref, q_ref, k_ref, v_ref, qg_ref, o_ref, qb_ref, acc_ref, carry_ref,
                      *, group):
    hg = pl.program_id(1)
    i = pl.program_id(2)
    tile = q_ref.shape[0]
    u = _suffix_ones(tile)
    row = lax.broadcasted_iota(jnp.int32, (tile, tile), 0)
    col = lax.broadcasted_iota(jnp.int32, (tile, tile), 1)
    for g in range(group):
        cols = slice(g * HEAD_DIM, (g + 1) * HEAD_DIM)
        qb_ref[:, cols] = _normed_query(q_ref[:, cols], qg_ref[...]).astype(BF16)
    acc_ref[...] = jnp.zeros_like(acc_ref)
    carry_ref[...] = jnp.zeros_like(carry_ref)

    def key_tile(j, mask):
        keys = pl.ds(pl.multiple_of(j * tile, tile), tile)
        heads = [slice(g * HEAD_DIM, (g + 1) * HEAD_DIM) for g in range(group)]
        zs = [_nt_dot(qb_ref[:, cols], k_ref[keys, cols]) + bias_ref[hg * group + g] * LOG2_E
              for g, cols in enumerate(heads)]
        keeps = [_sb_log_keep(z, mask) for z in zs]
        tails = [_sb_tail(lk, carry_ref[g], u) for g, (lk, _) in enumerate(keeps)]
        for g, cols in enumerate(heads):
            lk, log_beta = keeps[g]
            a = jnp.exp2(log_beta - tails[g])
            if mask is not None:
                a = jnp.where(mask, a, 0.0)
            acc_ref[:, cols] += jnp.dot(a.astype(BF16), v_ref[keys, cols],
                                        preferred_element_type=F32)
            carry_ref[g] += jnp.sum(lk, axis=-1, keepdims=True)

    key_tile(i, col < row)

    def body(step, _):
        key_tile(i - 1 - step, None)
        return 0

    lax.fori_loop(0, i, body, 0)
    o_ref[...] = acc_ref[...].astype(o_ref.dtype)


def _sb_prompt(qkv3, kb, vb, q_gain, bias, layer, batch):
    _, m, d = qkv3.shape
    t = m // batch
    heads = d // HEAD_DIM
    tile = _pick(t, SB_TILE)
    nq = t // tile
    group = SB_HEADS_PER_STEP if heads % SB_HEADS_PER_STEP == 0 else 1
    width = group * HEAD_DIM
    kv_spec = pl.BlockSpec((t, width), lambda b, h, i: (b, h))
    return pl.pallas_call(
        functools.partial(_sb_prompt_kernel, group=group),
        out_shape=jax.ShapeDtypeStruct((m, d), BF16),
        grid=(batch, heads // group, nq),
        in_specs=[pl.BlockSpec(memory_space=pltpu.SMEM),
                  pl.BlockSpec((None, tile, width), lambda b, h, i: (0, b * nq + i, h)),
                  kv_spec,
                  kv_spec,
                  pl.BlockSpec((None, 1, HEAD_DIM), lambda b, h, i: (layer, 0, 0))],
        out_specs=pl.BlockSpec((tile, width), lambda b, h, i: (b * nq + i, h)),
        scratch_shapes=[pltpu.VMEM((tile, width), BF16),
                        pltpu.VMEM((tile, width), F32),
                        pltpu.VMEM((group, tile, 1), F32)],
        compiler_params=_params("arbitrary", "arbitrary", "arbitrary"),
        name="sb_prompt",
    )(bias[layer], qkv3, kb, vb, q_gain[:, None, :])


def _sb_sample_kernel(pt_ref, q_ref, kn_ref, vn_ref, qg_ref, bias_ref, hmask_ref, wsuf_ref, *rest,
                      heads, tq, page, pages_per_step):
    del pt_ref
    kc_refs = rest[:pages_per_step]
    vc_refs = rest[pages_per_step:2 * pages_per_step]
    o_ref, qall_ref, acc_ref, carry_ref, kpad_ref, vpad_ref = rest[2 * pages_per_step:]
    p = pl.program_id(1)
    lanes = page * heads
    n_blk = lanes // HEAD_DIM
    n_grp = heads * tq // SUBLANES

    def tiles(kv_refs, valid):
        logits = []
        for k_ref, _ in kv_refs:
            f = _nt_dot(qall_ref[...].astype(BF16), k_ref[...].astype(BF16)) * hmask_ref[...]
            zs = f[0:SUBLANES]
            for g in range(1, n_grp):
                zs = zs + f[g * SUBLANES:(g + 1) * SUBLANES]
            z = zs
            for k in range(1, SUBLANES // tq):
                z = z + pltpu.roll(zs, k * tq, axis=0)
            z = z + jnp.concatenate([bias_ref[...] * LOG2_E] * n_blk, axis=1)
            logits.append(_sb_log_keep(z, valid))
        sums = []
        for lk, _ in logits:
            blocks = jnp.concatenate(
                [lk[:, b * HEAD_DIM:(b + 1) * HEAD_DIM] for b in range(n_blk)], axis=0)
            sums.append(jnp.dot(_split_lanes(blocks), wsuf_ref[...], preferred_element_type=F32))
        run = carry_ref[...]
        out = jnp.zeros(acc_ref.shape, F32)
        for (_, log_beta), res, (_, v_ref) in zip(logits, sums, kv_refs):
            tails = [None] * n_blk
            for b in reversed(range(n_blk)):
                blk = res[b * SUBLANES:(b + 1) * SUBLANES]
                tails[b] = blk[:, :HEAD_DIM] + run
                run = run + blk[:, HEAD_DIM:]
            a = jnp.exp2(log_beta - jnp.concatenate(tails, axis=1))
            if valid is not None:
                a = jnp.where(valid, a, 0.0)
            a_heads = (jnp.concatenate([a] * n_grp, axis=0) * hmask_ref[...]).astype(BF16)
            out = out + jnp.dot(a_heads, v_ref[...].astype(BF16), preferred_element_type=F32)
        carry_ref[...] = run
        acc_ref[...] += out

    @pl.when(p == 0)
    def _new_tokens():
        for h in range(heads):
            cols = slice(h * HEAD_DIM, (h + 1) * HEAD_DIM)
            qall_ref[h * tq:(h + 1) * tq, :] = _normed_query(q_ref[:, cols], qg_ref[...])
        kpad_ref[...] = jnp.zeros_like(kpad_ref)
        vpad_ref[...] = jnp.zeros_like(vpad_ref)
        kpad_ref[0:tq * heads, :] = kn_ref[...]
        vpad_ref[0:tq * heads, :] = vn_ref[...]
        acc_ref[...] = jnp.zeros_like(acc_ref)
        carry_ref[...] = jnp.zeros_like(carry_ref)
        query = lax.broadcasted_iota(jnp.int32, (SUBLANES, lanes), 0) % tq
        position = lax.broadcasted_iota(jnp.int32, (SUBLANES, lanes), 1) // heads
        tiles([(kpad_ref, vpad_ref)], position < query)

    @pl.when(p > 0)
    def _past_page():
        tiles(list(zip(kc_refs, vc_refs)), None)

    @pl.when(p == pl.num_programs(1) - 1)
    def _finish():
        for h in range(heads):
            o_ref[:, h * HEAD_DIM:(h + 1) * HEAD_DIM] = acc_ref[h * tq:(h + 1) * tq, :]


def _sb_sample(qkv3, kn, vh, q_gain, bias, cache_k, cache_v, page_table, layer, batch):
    _, m, d = qkv3.shape
    tq = m // batch
    heads = d // HEAD_DIM
    assert SUBLANES % tq == 0 and (heads * tq) % SUBLANES == 0 and HEAD_DIM % heads == 0
    rows = heads * tq
    n_pages = page_table.shape[1]
    lanes = cache_k.shape[2]
    page = lanes // heads
    qkv4 = qkv3.reshape(3, batch, tq, d)
    lane_head = np.arange(lanes) % heads
    bias_tile = jnp.broadcast_to(jnp.tile(bias[layer], HEAD_DIM // heads)[None, :],
                                 (SUBLANES, HEAD_DIM)).astype(F32)
    hmask = jnp.asarray(lane_head[None, :] == (np.arange(rows) // tq)[:, None], F32)
    l = np.arange(HEAD_DIM)
    same_head = (l[:, None] % heads) == (l[None, :] % heads)
    later = (l[:, None] // heads) > (l[None, :] // heads)
    wsuf = np.concatenate([same_head & later, same_head], axis=1)
    wsuf = jnp.asarray(np.concatenate([wsuf] * SPLIT_PARTS, axis=0), BF16)

    pps = SB_PAGES_PER_STEP if n_pages % SB_PAGES_PER_STEP == 0 else 1

    def page_index(j):
        return lambda b, p, pt: (layer, pt[b, n_pages - 1 - ((jnp.maximum(p, 1) - 1) * pps + j)], 0, 0)

    page_specs = [pl.BlockSpec((None, None, lanes, HEAD_DIM), page_index(j)) for j in range(pps)]
    fixed = lambda shape: pl.BlockSpec(shape, lambda b, p, pt: (0,) * len(shape))
    new_spec = pl.BlockSpec((None, None, tq * heads, HEAD_DIM), lambda b, p, pt: (layer, b, 0, 0))
    stack_shape = (kn.shape[0], batch, tq * heads, HEAD_DIM)
    grid_spec = pltpu.PrefetchScalarGridSpec(
        num_scalar_prefetch=1,
        grid=(batch, n_pages // pps + 1),
        in_specs=[pl.BlockSpec((None, None, tq, d), lambda b, p, pt: (0, b, 0, 0)),
                  new_spec,
                  new_spec,
                  pl.BlockSpec((None, 1, HEAD_DIM), lambda b, p, pt: (layer, 0, 0)),
                  fixed(bias_tile.shape),
                  fixed(hmask.shape),
                  fixed(wsuf.shape)] + page_specs + page_specs,
        out_specs=pl.BlockSpec((None, tq, d), lambda b, p, pt: (b, 0, 0)),
        scratch_shapes=[pltpu.VMEM((rows, HEAD_DIM), F32),
                        pltpu.VMEM((rows, HEAD_DIM), F32),
                        pltpu.VMEM((SUBLANES, HEAD_DIM), F32),
                        pltpu.VMEM((lanes, HEAD_DIM), F32),
                        pltpu.VMEM((lanes, HEAD_DIM), F32)])
    out = pl.pallas_call(
        functools.partial(_sb_sample_kernel, heads=heads, tq=tq, page=page, pages_per_step=pps),
        out_shape=jax.ShapeDtypeStruct((batch, tq, d), F32),
        grid_spec=grid_spec,
        compiler_params=_params("arbitrary", "arbitrary"),
        name="sb_sample",
    )(page_table, qkv4, kn.reshape(stack_shape), vh.reshape(stack_shape),
      q_gain[:, None, :], bias_tile, hmask, wsuf, *([cache_k] * pps), *([cache_v] * pps))
    return out.reshape(m, d)


def _lower_bounds_kernel(x_ref, lb_ref):
    x = x_ref[...]
    e = jnp.exp(x - jnp.max(x, axis=0, keepdims=True))
    soft = e / jnp.sum(e, axis=0, keepdims=True)
    depth = x.shape[0]
    run = jnp.zeros_like(soft[0:1])
    for layer in range(depth):
        run = run + soft[layer:layer + 1]
        lb = run - soft[0:1]
        lb_ref[layer, 0:1, :] = lb
        lb_ref[layer, 1:2, :] = jnp.log(lb)
        lb_ref[layer, 2:3, :] = jnp.log1p(-lb)
        lb_ref[layer, 3:4, :] = 1.0 - lb
        lb_ref[layer, 4:8, :] = jnp.zeros((4, x.shape[1]), F32)


def _lower_bounds(hg_lower_bounds):
    depth, d = hg_lower_bounds.shape
    return pl.pallas_call(
        _lower_bounds_kernel,
        out_shape=jax.ShapeDtypeStruct((depth, SUBLANES, d), F32),
        name="hg_lower_bounds",
    )(hg_lower_bounds.astype(F32))


def _hg_gates(f, lbp):
    log_lb, log_1m, one_m = lbp[1:2], lbp[2:3], lbp[3:4]
    log_sig = jnp.minimum(f, 0.0) - jnp.log(1.0 + jnp.exp(-jnp.abs(f)))
    b = log_1m + log_sig
    g = jnp.maximum(log_lb, b) + jnp.log(1.0 + jnp.exp(-jnp.abs(log_lb - b)))
    key = one_m * (1.0 / (1.0 + jnp.exp(f)))
    return g, key


def _hg_output(o, gate, gain):
    ms = jnp.mean(o * o, axis=-1, keepdims=True)
    on = o * lax.rsqrt(ms + NORM_EPS) * gain
    return on * (gate * (1.0 / (1.0 + jnp.exp(-gate))))


def _hg_prompt_kernel(q_ref, f_ref, i_ref, g_ref, lbp_ref, gain_ref, o_ref, s_ref,
                      st_ref, gc_ref, *, chunk, sub, group):
    t = q_ref.shape[0]
    n_sub = chunk // sub
    n_chunks = t // chunk
    gain = gain_ref[...]
    r = lax.broadcasted_iota(jnp.int32, (chunk, chunk), 0)
    c = lax.broadcasted_iota(jnp.int32, (chunk, chunk), 1)
    lower = jnp.where(c <= r, 1.0, 0.0).astype(BF16)
    crow = lax.broadcasted_iota(jnp.int32, (chunk, HEAD_DIM), 0)
    srow = lax.broadcasted_iota(jnp.int32, (sub, HEAD_DIM), 0)
    own = (lax.broadcasted_iota(jnp.int32, (sub, sub * sub), 1) // sub
           == lax.broadcasted_iota(jnp.int32, (sub, sub * sub), 0))
    st_ref[...] = jnp.zeros_like(st_ref)

    def head_cols(hi):
        return slice(hi * HEAD_DIM, (hi + 1) * HEAD_DIM)

    def decays(ci):
        rows = pl.ds(pl.multiple_of(ci * chunk, chunk), chunk)
        gates = [_hg_gates(f_ref[rows, head_cols(hi)], lbp_ref[:, head_cols(hi)])
                 for hi in range(group)]
        parts = [p for glog, _ in gates for p in _split_bf16(glog * LOG2_E, 3)]
        sums = jnp.dot(lower, jnp.concatenate(parts, axis=1), preferred_element_type=F32)
        out = []
        for hi, (_, kk) in enumerate(gates):
            gc = sums[:, head_cols(3 * hi)] + sums[:, head_cols(3 * hi + 1)] + sums[:, head_cols(3 * hi + 2)]
            out.append((gc, kk))
        return tuple(out)

    def scores(r0, hi, gc, kk):
        rows = pl.ds(r0, chunk)
        gc_ref[hi] = gc
        q = q_ref[rows, head_cols(hi)]
        vb = i_ref[rows, head_cols(hi)].astype(BF16)
        st = st_ref[hi]
        inter = _nt_dot((q * jnp.exp2(gc)).astype(BF16), st.astype(BF16))
        g_last = gc[chunk - 1:chunk, :]
        offs, diags = [], []
        for si in range(n_sub):
            base = si * sub
            blk = slice(base, base + sub)
            q_s, g_s, k_s = q[blk], gc[blk], kk[blk]
            if si > 0:
                g_start = gc[base - 1:base, :]
                k_prev = jnp.where(crow < base, kk * jnp.exp2(g_start - gc), 0.0)
                q_in = q_s * jnp.exp2(g_s - g_start)
                offs.append(_nt_dot(q_in.astype(BF16), k_prev.astype(BF16)))
            else:
                offs.append(None)
            stacked = []
            for ti in range(sub):
                g_t = gc_ref[hi, base + ti:base + ti + 1, :]
                k_t = jnp.where(srow <= ti, k_s * jnp.exp2(g_t - g_s), 0.0)
                stacked.append(k_t.astype(BF16))
            diags.append(_nt_dot(q_s.astype(BF16), jnp.concatenate(stacked, axis=0)))
        k_dec = kk * jnp.exp2(g_last - gc)
        st_ref[hi] = st * jnp.exp2(g_last) + lax.dot_general(
            vb, k_dec.astype(BF16), (((0,), (0,)), ((), ())), preferred_element_type=F32)
        return vb, inter, offs, diags

    def outputs(r0, hi, vb, inter, offs, diags):
        for si in range(n_sub):
            base = si * sub
            blk = slice(base, base + sub)
            o_s = inter[blk]
            if offs[si] is not None:
                o_s = o_s + jnp.dot(offs[si].astype(BF16), vb, preferred_element_type=F32)
            sc = jnp.where(own, diags[si], 0.0).astype(BF16)
            o_s = o_s + jnp.dot(sc, jnp.concatenate([vb[blk]] * sub, axis=0),
                                preferred_element_type=F32)
            out_rows = pl.ds(pl.multiple_of(r0 + base, sub), sub)
            o_ref[out_rows, head_cols(hi)] = _hg_output(
                o_s, g_ref[out_rows, head_cols(hi)], gain).astype(o_ref.dtype)

    def body(ci, current):
        r0 = pl.multiple_of(ci * chunk, chunk)
        following = decays(jnp.minimum(ci + 1, n_chunks - 1))
        staged = [scores(r0, hi, *current[hi]) for hi in range(group)]
        for hi in range(group):
            outputs(r0, hi, *staged[hi])
        return following

    lax.fori_loop(0, n_chunks, body, decays(0))
    for hi in range(group):
        s_ref[hi] = st_ref[hi].T


def _hg_prompt(proj, lbp, out_gain, layer, hg_index, batch):
    m, d4 = proj.shape
    d = d4 // 4
    heads = d // HEAD_DIM
    t = m // batch
    chunk = _pick(t, HG_CHUNK, HG_SUB)
    sub = HG_SUB
    group = HG_HEADS_PER_STEP if heads % HG_HEADS_PER_STEP == 0 else 1
    width = group * HEAD_DIM
    ng = heads // group
    col = lambda sec: (lambda b, h: (b, sec * ng + h))
    return pl.pallas_call(
        functools.partial(_hg_prompt_kernel, chunk=chunk, sub=sub, group=group),
        out_shape=(jax.ShapeDtypeStruct((m, d), BF16),
                   jax.ShapeDtypeStruct((batch, heads, HEAD_DIM, HEAD_DIM), F32)),
        grid=(batch, ng),
        in_specs=[pl.BlockSpec((t, width), col(0)),
                  pl.BlockSpec((t, width), col(1)),
                  pl.BlockSpec((t, width), col(2)),
                  pl.BlockSpec((t, width), col(3)),
                  pl.BlockSpec((None, SUBLANES, width), lambda b, h: (layer, 0, h)),
                  pl.BlockSpec((None, 1, HEAD_DIM), lambda b, h: (hg_index, 0, 0))],
        out_specs=(pl.BlockSpec((t, width), lambda b, h: (b, h)),
                   pl.BlockSpec((None, group, HEAD_DIM, HEAD_DIM), lambda b, h: (b, h, 0, 0))),
        scratch_shapes=[pltpu.VMEM((group, HEAD_DIM, HEAD_DIM), F32),
                        pltpu.VMEM((group, chunk, HEAD_DIM), F32)],
        compiler_params=_params("arbitrary", "arbitrary"),
        name="hg_prompt",
    )(proj, proj, proj, proj, lbp, out_gain[:, None, :])


def _hg_sample_kernel(q_ref, f_ref, i_ref, g_ref, lbp_ref, gain_ref, s0_ref, o_ref, s_ref, pad_ref,
                      *, heads):
    tq = q_ref.shape[0]
    pad_ref[...] = jnp.zeros_like(pad_ref)

    def columns(slot, x):
        pad_ref[slot, 0:tq, :] = x
        return pad_ref[slot].T

    for h in range(heads):
        cols = slice(h * HEAD_DIM, (h + 1) * HEAD_DIM)
        glog, kk = _hg_gates(f_ref[:, cols], lbp_ref[:, cols])
        q_c = columns(3 * h, q_ref[:, cols])
        f_c = columns(3 * h + 1, jnp.exp(glog))
        k_c = columns(3 * h + 2, kk)
        v = i_ref[:, cols]
        s = s0_ref[h]
        for ti in range(tq):
            s = f_c[:, ti:ti + 1] * s + k_c[:, ti:ti + 1] * v[ti:ti + 1, :]
            o = jnp.sum(q_c[:, ti:ti + 1] * s, axis=0, keepdims=True)
            o_ref[ti:ti + 1, cols] = _hg_output(o, g_ref[ti:ti + 1, cols], gain_ref[...])
        s_ref[h] = s


def _hg_sample(proj, lbp, out_gain, state, layer, hg_index, batch):
    m, d4 = proj.shape
    d = d4 // 4
    heads = d // HEAD_DIM
    tq = m // batch
    proj3 = proj.reshape(batch, tq, d4)
    col = lambda sec: (lambda b: (b, 0, sec))
    state_block = (None, heads, HEAD_DIM, HEAD_DIM)
    out, s_new = pl.pallas_call(
        functools.partial(_hg_sample_kernel, heads=heads),
        out_shape=(jax.ShapeDtypeStruct((batch, tq, d), F32),
                   jax.ShapeDtypeStruct((batch, heads, HEAD_DIM, HEAD_DIM), F32)),
        grid=(batch,),
        in_specs=[pl.BlockSpec((None, tq, d), col(0)),
                  pl.BlockSpec((None, tq, d), col(1)),
                  pl.BlockSpec((None, tq, d), col(2)),
                  pl.BlockSpec((None, tq, d), col(3)),
                  pl.BlockSpec((None, SUBLANES, d), lambda b: (layer, 0, 0)),
                  pl.BlockSpec((None, 1, HEAD_DIM), lambda b: (hg_index, 0, 0)),
                  pl.BlockSpec((None,) + state_block, lambda b: (hg_index, b, 0, 0, 0))],
        out_specs=(pl.BlockSpec((None, tq, d), lambda b: (b, 0, 0)),
                   pl.BlockSpec(state_block, lambda b: (b, 0, 0, 0))),
        scratch_shapes=[pltpu.VMEM((3 * heads, HEAD_DIM, HEAD_DIM), F32)],
        compiler_params=_params("arbitrary"),
        name="hg_sample",
    )(proj3, proj3, proj3, proj3, lbp, out_gain[:, None, :], state)
    return out.reshape(m, d), s_new


def _ffn_gate_kernel(a_ref, b_ref, buf_ref, w_ref, cb_ref, o_ref, st_ref, ext_ref):
    t = a_ref.shape[0]
    ext_ref[CONV_LEAD:SUBLANES, :] = buf_ref[...]
    o_ref[...] = _conv_gate(ext_ref, a_ref[...], b_ref[...], w_ref, cb_ref).astype(o_ref.dtype)
    st_ref[...] = ext_ref[SUBLANES + t - (FFN_CONV_W - 1):SUBLANES + t, :]


CONV_LEAD = SUBLANES - (FFN_CONV_W - 1)


def _conv_gate(ext_ref, a, b, w_ref, cb_ref):
    t = a.shape[0]
    lead = CONV_LEAD
    ext_ref[SUBLANES:SUBLANES + t, :] = a
    c = cb_ref[...]
    for j in range(FFN_CONV_W - 1):
        c = c + w_ref[j:j + 1, :] * ext_ref[lead + j:lead + j + t, :]
    c = c + w_ref[FFN_CONV_W - 1:FFN_CONV_W, :] * a
    return c * (1.0 / (1.0 + jnp.exp(-c))) * b


def _ffn_in_kernel(x_ref, wa_ref, wb_ref, xs_ref, cw_ref, cb_ref, o_ref, st_ref, as_ref, bs_ref,
                   wa16_ref, wb16_ref, hist_ref, *, tiles_per_seq, row_blocks):
    i = pl.program_id(1)
    tm = x_ref.shape[0]

    @pl.when(i == 0)
    def _():
        wa16_ref[...] = wa_ref[...].astype(BF16)
        wb16_ref[...] = wb_ref[...].astype(BF16)
        xs = xs_ref[...].astype(BF16)
        as_ref[...] = jnp.dot(xs, wa16_ref[...], preferred_element_type=F32)
        bs_ref[...] = jnp.dot(xs, wb16_ref[...], preferred_element_type=F32)

    @pl.when(i % tiles_per_seq == 0)
    def _():
        hist_ref[...] = jnp.zeros_like(hist_ref)

    rb = tm // row_blocks
    row = lax.broadcasted_iota(jnp.int32, (SUBLANES, o_ref.shape[1]), 0)

    def products(r):
        x = x_ref[r * rb:(r + 1) * rb, :]
        return (jnp.dot(x, wa16_ref[...], preferred_element_type=F32),
                jnp.dot(x, wb16_ref[...], preferred_element_type=F32))

    def shifted(a, first_rows):
        k = len(first_rows)
        moved = pltpu.roll(a, k, axis=0)
        top = moved[0:SUBLANES]
        for j, r in enumerate(first_rows):
            top = jnp.where(row == j, r, top)
        return jnp.concatenate([top, moved[SUBLANES:]], axis=0)

    def gate(r, a, b, before):
        c = cb_ref[...]
        c = c + cw_ref[0:1, :] * shifted(a, [before[0:1], before[1:2]])
        c = c + cw_ref[1:2, :] * shifted(a, [before[1:2]])
        c = c + cw_ref[2:3, :] * a
        o_ref[r * rb:(r + 1) * rb, :] = (c * (1.0 / (1.0 + jnp.exp(-c))) * b).astype(o_ref.dtype)

    before = hist_ref[...]
    pending = products(0)
    for r in range(1, row_blocks):
        issued = products(r)
        gate(r - 1, *pending, before)
        before = pending[0][rb - (FFN_CONV_W - 1):rb, :]
        pending = issued
    gate(row_blocks - 1, *pending, before)
    last = pending[0][rb - (FFN_CONV_W - 1):rb, :]
    hist_ref[...] = last
    st_ref[...] = last


def _ffn_in(x, xs, w_in, conv_w, conv_b, layer, batch):
    m, k = x.shape
    ms = xs.shape[0]
    f = w_in.shape[-1] // 2
    t = m // batch
    tm = _pick(t, 1024, SUBLANES)
    tps = t // tm
    tf = _pick(f, 512)
    nf = f // tf
    return pl.pallas_call(
        functools.partial(_ffn_in_kernel, tiles_per_seq=tps,
                          row_blocks=FFN_ROW_BLOCKS if tm % (FFN_ROW_BLOCKS * SUBLANES) == 0 else 1),
        out_shape=(jax.ShapeDtypeStruct((m, f), BF16),
                   jax.ShapeDtypeStruct((batch, FFN_CONV_W - 1, f), F32),
                   jax.ShapeDtypeStruct((ms, f), F32),
                   jax.ShapeDtypeStruct((ms, f), F32)),
        grid=(nf, m // tm),
        in_specs=[pl.BlockSpec((tm, k), lambda j, i: (i, 0)),
                  pl.BlockSpec((None, k, tf), lambda j, i: (layer, 0, j)),
                  pl.BlockSpec((None, k, tf), lambda j, i: (layer, 0, nf + j)),
                  pl.BlockSpec((ms, k), lambda j, i: (0, 0)),
                  pl.BlockSpec((None, FFN_CONV_W, tf), lambda j, i: (layer, 0, j)),
                  pl.BlockSpec((None, 1, tf), lambda j, i: (layer, 0, j))],
        out_specs=(pl.BlockSpec((tm, tf), lambda j, i: (i, j)),
                   pl.BlockSpec((None, FFN_CONV_W - 1, tf), lambda j, i: (i // tps, 0, j)),
                   pl.BlockSpec((ms, tf), lambda j, i: (0, j)),
                   pl.BlockSpec((ms, tf), lambda j, i: (0, j))),
        scratch_shapes=[pltpu.VMEM((k, tf), BF16),
                        pltpu.VMEM((k, tf), BF16),
                        pltpu.VMEM((FFN_CONV_W - 1, tf), F32)],
        compiler_params=_params("arbitrary", "arbitrary"),
        name="ffn_in",
    )(x, w_in, w_in, xs, conv_w, conv_b[:, None, :])


def _ffn_gate(a, b, buf, conv_w, conv_b, layer, batch):
    m, f = a.shape
    t = m // batch
    row_spec = pl.BlockSpec((None, t, f), lambda i: (i, 0, 0))
    hist_spec = pl.BlockSpec((None, FFN_CONV_W - 1, f), lambda i: (i, 0, 0))
    out, st = pl.pallas_call(
        _ffn_gate_kernel,
        out_shape=(jax.ShapeDtypeStruct((batch, t, f), F32),
                   jax.ShapeDtypeStruct((batch, FFN_CONV_W - 1, f), F32)),
        grid=(batch,),
        in_specs=[row_spec,
                  row_spec,
                  pl.BlockSpec((None, None, FFN_CONV_W - 1, f), lambda i: (layer, i, 0, 0)),
                  pl.BlockSpec((None, FFN_CONV_W, f), lambda i: (layer, 0, 0)),
                  pl.BlockSpec((None, 1, f), lambda i: (layer, 0, 0))],
        out_specs=(row_spec, hist_spec),
        scratch_shapes=[pltpu.VMEM((t + SUBLANES, f), F32)],
        compiler_params=_params("arbitrary"),
        name="ffn_gate",
    )(a.reshape(batch, t, f), b.reshape(batch, t, f), buf, conv_w, conv_b[:, None, :])
    return out.reshape(m, f), st


def kernel(x_prompt, x_sample, cache_sb_k, cache_sb_v, page_table, state_hgrn, state_ffn_conv,
           norm_mixer, norm_ffn, w_sb_qkv, sb_q_gain, sb_k_gain, sb_logit_bias, w_sb_o,
           w_hg_in, hg_lower_bounds, hg_out_gain, w_hg_o,
           w_ffn_in, ffn_conv_w, ffn_conv_b, w_ffn_out):
    bp, tp, d = x_prompt.shape
    bs, ts, _ = x_sample.shape
    depth = norm_mixer.shape[0]
    heads = d // HEAD_DIM
    n_mixers = 2
    n_sb, pool, page = cache_sb_k.shape[:3]
    cache_k = cache_sb_k.reshape(n_sb, pool, page * heads, HEAD_DIM)
    cache_v = cache_sb_v.reshape(n_sb, pool, page * heads, HEAD_DIM)
    lbp = _lower_bounds(hg_lower_bounds)

    xp = x_prompt.reshape(bp * tp, d)
    xs = x_sample.reshape(bs * ts, d)
    kv_p = kv_s = None
    sp_l, ss_l, cp_l, cs_l = [], [], [], []
    for layer in range(depth):
        j = layer // n_mixers
        hp = _rmsnorm(xp, norm_mixer, layer)
        hs = _rmsnorm(xs, norm_mixer, layer)
        if layer % n_mixers == 0:
            qkv_p, qkv_s = _matmul(hp, hs, w_sb_qkv, j, sections=3)
            kb_p, vb_p, *kv_p = _kv_heads(qkv_p, sb_k_gain, j, n_sb, kv_p)
            _, _, *kv_s = _kv_heads(qkv_s, sb_k_gain, j, n_sb, kv_s)
            op = _sb_prompt(qkv_p, kb_p, vb_p, sb_q_gain, sb_logit_bias, j, bp)
            os_ = _sb_sample(qkv_s, kv_s[0], kv_s[1], sb_q_gain, sb_logit_bias, cache_k, cache_v,
                             page_table, j, bs)
            xp, xs = _matmul(op, os_, w_sb_o, j, res=xp, ress=xs)
        else:
            proj_p, proj_s = _matmul(hp, hs, w_hg_in, j)
            op, sp = _hg_prompt(proj_p[0], lbp, hg_out_gain, layer, j, bp)
            os_, ss = _hg_sample(proj_s[0], lbp, hg_out_gain, state_hgrn, layer, j, bs)
            xp, xs = _matmul(op, os_, w_hg_o, j, res=xp, ress=xs)
            sp_l.append(sp)
            ss_l.append(ss)
        xp, xs = xp[0], xs[0]
        hp = _rmsnorm(xp, norm_ffn, layer)
        hs = _rmsnorm(xs, norm_ffn, layer)
        gp, cp, a_s, b_s = _ffn_in(hp, hs, w_ffn_in, ffn_conv_w, ffn_conv_b, layer, bp)
        gs, cs = _ffn_gate(a_s, b_s, state_ffn_conv, ffn_conv_w, ffn_conv_b, layer, bs)
        xp, xs = _matmul(gp, gs, w_ffn_out, layer, res=xp, ress=xs, tm_target=512, tn_target=512)
        xp, xs = xp[0], xs[0]
        cp_l.append(cp)
        cs_l.append(cs)

    kv_shape_p = (n_sb, bp, tp, heads, HEAD_DIM)
    kv_shape_s = (n_sb, bs, ts, heads, HEAD_DIM)
    return (xp.reshape(bp, tp, d), xs.reshape(bs, ts, d),
            kv_p[0].reshape(kv_shape_p), kv_p[1].reshape(kv_shape_p),
            kv_s[0].reshape(kv_shape_s), kv_s[1].reshape(kv_shape_s),
            jnp.stack(sp_l), jnp.stack(ss_l), jnp.stack(cp_l), jnp.stack(cs_l))
```
